```python
import jax, jax.numpy as jnp
from jax import lax
import numpy as np

D_MODEL = 1024
BATCH = 16
SEQ = 4096
DEPTH = 1

MLA_HEADS = 8
MLA_NOPE_DIM = 64
MLA_ROPE_DIM = 32
MLA_V_DIM = 64
MLA_Q_RANK = 256
MLA_KV_RANK = 128
MLA_WIDTH = MLA_HEADS * MLA_V_DIM
MOBA_HEADS = 8
MOBA_HEAD_DIM = 64
MOBA_WIDTH = MOBA_HEADS * MOBA_HEAD_DIM
MOBA_BLOCK = 256
MOBA_TOPK = 3
MOBA_Q_CHUNK = 16
ATTN_Q_BLOCK = 128
ROPE_THETA = 10000.0
EPS = 1e-6
N_BRANCHES = 2
IN_SIZES = (MLA_Q_RANK, MLA_KV_RANK, MLA_ROPE_DIM, MLA_WIDTH,
            MOBA_WIDTH, MOBA_WIDTH, MOBA_WIDTH, MOBA_WIDTH, N_BRANCHES * D_MODEL)
D_IN = MLA_Q_RANK + MLA_KV_RANK + MLA_ROPE_DIM + MLA_WIDTH + 4 * MOBA_WIDTH + N_BRANCHES * D_MODEL

kernel_name = "hybrid_mla_moba_gated_parallel"


def rms_norm(t, g):
    tf = t.astype(jnp.float32)
    tf = tf * lax.rsqrt(jnp.mean(tf * tf, axis=-1, keepdims=True) + EPS)
    return (tf * g.astype(jnp.float32)).astype(t.dtype)


def apply_rope(t, positions):
    half = t.shape[-1] // 2
    inv_freq = ROPE_THETA ** (-jnp.arange(half, dtype=jnp.float32) / half)
    ang = positions.astype(jnp.float32)[:, :, None, None] * inv_freq
    cos, sin = jnp.cos(ang), jnp.sin(ang)
    tf = t.astype(jnp.float32)
    t1, t2 = tf[..., :half], tf[..., half:]
    return jnp.concatenate([t1 * cos - t2 * sin, t2 * cos + t1 * sin], axis=-1).astype(t.dtype)


def split_cols(t, sizes):
    outs, start = [], 0
    for n in sizes:
        outs.append(t[..., start:start + n])
        start += n
    return outs


def causal_dense_attention(q, k, v, scale):
    B, S, H, _ = q.shape
    kpos = jnp.arange(S)

    def one_block(i):
        start = i * ATTN_Q_BLOCK
        qb = lax.dynamic_slice_in_dim(q, start, ATTN_Q_BLOCK, axis=1)
        s = jnp.einsum('bqhd,bkhd->bhqk', qb, k).astype(jnp.float32) * scale
        qpos = start + jnp.arange(ATTN_Q_BLOCK)
        s = jnp.where(kpos[None, :] <= qpos[:, None], s, -jnp.inf)
        p = jax.nn.softmax(s, axis=-1).astype(v.dtype)
        return jnp.einsum('bhqk,bkhd->bqhd', p, v)

    out = lax.map(one_block, jnp.arange(S // ATTN_Q_BLOCK))
    return out.transpose(1, 0, 2, 3, 4).reshape(B, S, H, v.shape[-1])


def moba_attention(q, k, v):
    B, S, H, dh = q.shape
    nb = -(-S // MOBA_BLOCK)
    n_top = min(MOBA_TOPK, nb)
    pad = nb * MOBA_BLOCK - S
    kp = jnp.pad(k, ((0, 0), (0, pad), (0, 0), (0, 0)))
    vp = jnp.pad(v, ((0, 0), (0, pad), (0, 0), (0, 0)))
    kb = kp.reshape(B, nb, MOBA_BLOCK, H, dh).transpose(0, 3, 1, 2, 4)
    vb = vp.reshape(B, nb, MOBA_BLOCK, H, dh).transpose(0, 3, 1, 2, 4)
    kmean = jnp.mean(kb.astype(jnp.float32), axis=3)
    scale = dh ** -0.5
    bi = jnp.arange(B)[:, None, None, None]
    hi = jnp.arange(H)[None, :, None, None]
    in_blk = jnp.arange(MOBA_BLOCK)
    blk_ids = jnp.arange(nb)
    n_sel = n_top * MOBA_BLOCK

    def one_chunk(i):
        start = i * MOBA_Q_CHUNK
        qc = lax.dynamic_slice_in_dim(q, start, MOBA_Q_CHUNK, axis=1)
        qpos = start + jnp.arange(MOBA_Q_CHUNK)
        own = start // MOBA_BLOCK
        gate = jnp.einsum('bqhd,bhnd->bhqn', qc.astype(jnp.float32), kmean)
        gate = jnp.where(blk_ids < own, gate, -jnp.inf)
        _, idx = lax.top_k(gate, n_top)
        valid = jnp.arange(n_top) < own
        k_sel = kb[bi, hi, idx]
        v_sel = vb[bi, hi, idx]
        s_sel = jnp.einsum('bqhd,bhqnkd->bhqnk', qc, k_sel).astype(jnp.float32) * scale
        s_sel = jnp.where(valid[:, None], s_sel, -jnp.inf)
        k_own = lax.dynamic_index_in_dim(kb, own, axis=2, keepdims=False)
        v_own = lax.dynamic_index_in_dim(vb, own, axis=2, keepdims=False)
        s_own = jnp.einsum('bqhd,bhkd->bhqk', qc, k_own).astype(jnp.float32) * scale
        kpos_own = own * MOBA_BLOCK + in_blk
        s_own = jnp.where(kpos_own[None, :] <= qpos[:, None], s_own, -jnp.inf)
        s = jnp.concatenate([s_sel.reshape(B, H, MOBA_Q_CHUNK, n_sel), s_own], axis=-1)
        p = jax.nn.softmax(s, axis=-1).astype(v.dtype)
        p_sel = p[..., :n_sel].reshape(B, H, MOBA_Q_CHUNK, n_top, MOBA_BLOCK)
        p_own = p[..., n_sel:]
        return (jnp.einsum('bhqnk,bhqnkd->bqhd', p_sel, v_sel)
                + jnp.einsum('bhqk,bhkd->bqhd', p_own, v_own))

    out = lax.map(one_chunk, jnp.arange(S // MOBA_Q_CHUNK))
    return out.transpose(1, 0, 2, 3, 4).reshape(B, S, H, dh)


def setup_inputs(seed: int = 0) -> dict:
    key = jax.random.key(seed)
    ks = jax.random.split(key, 20)

    def w(k, shape, fan_in, mult=1.0):
        return jax.random.normal(k, shape, jnp.float32) * (mult * fan_in ** -0.5)

    def gain(k, n):
        return 1.0 + 0.05 * jax.random.normal(k, (DEPTH, n), jnp.float32)

    x = jax.random.normal(ks[0], (BATCH, SEQ, D_MODEL), jnp.float32)
    c = jax.random.normal(ks[1], (BATCH, D_MODEL), jnp.float32)
    offset = jax.random.randint(ks[2], (BATCH, 1), 0, 1024, dtype=jnp.int32)
    positions = (offset + jnp.arange(SEQ, dtype=jnp.int32)[None, :]).astype(jnp.int32)
    return {
        "x": x,
        "c": c,
        "positions": positions,
        "w_ada": w(ks[3], (DEPTH, D_MODEL, 3 * D_MODEL), D_MODEL, 0.5),
        "b_ada": 0.01 * jax.random.normal(ks[4], (DEPTH, 3 * D_MODEL), jnp.float32),
        "g_pre": gain(ks[5], D_MODEL),
        "g_post": gain(ks[6], D_MODEL),
        "w_in": w(ks[7], (DEPTH, D_MODEL, D_IN), D_MODEL),
        "g_q_lat": gain(ks[8], MLA_Q_RANK),
        "w_uq": w(ks[9], (DEPTH, MLA_Q_RANK, MLA_HEADS * (MLA_NOPE_DIM + MLA_ROPE_DIM)), MLA_Q_RANK),
        "g_kv_lat": gain(ks[10], MLA_KV_RANK),
        "w_ukv": w(ks[11], (DEPTH, MLA_KV_RANK, MLA_HEADS * (MLA_NOPE_DIM + MLA_V_DIM)), MLA_KV_RANK),
        "w_o_mla": w(ks[12], (DEPTH, MLA_WIDTH, D_MODEL), MLA_WIDTH),
        "w_o_moba": w(ks[13], (DEPTH, MOBA_WIDTH, D_MODEL), MOBA_WIDTH),
        "b_merge": 0.01 * jax.random.normal(ks[14], (DEPTH, N_BRANCHES * D_MODEL), jnp.float32),
        "w_out": w(ks[15], (DEPTH, D_MODEL, D_MODEL), D_MODEL),
    }


def reference(x, c, positions, w_ada, b_ada, g_pre, g_post, w_in, g_q_lat, w_uq,
              g_kv_lat, w_ukv, w_o_mla, w_o_moba, b_merge, w_out):
    B, S, D = x.shape
    for l in range(DEPTH):
        mod = jax.nn.silu(c) @ w_ada[l] + b_ada[l]
        shift, scale, gate = jnp.split(mod, 3, axis=-1)
        h = rms_norm(x, g_pre[l]) * (1 + scale[:, None, :]) + shift[:, None, :]

        proj = h @ w_in[l]
        (q_lat, kv_lat, k_rope, z_mla, q_mb, k_mb, v_mb, z_mb,
         merge_logits) = split_cols(proj, IN_SIZES)

        q = (rms_norm(q_lat, g_q_lat[l]) @ w_uq[l]).reshape(B, S, MLA_HEADS, MLA_NOPE_DIM + MLA_ROPE_DIM)
        q_nope, q_pe = q[..., :MLA_NOPE_DIM], q[..., MLA_NOPE_DIM:]
        kv = (rms_norm(kv_lat, g_kv_lat[l]) @ w_ukv[l]).reshape(B, S, MLA_HEADS, MLA_NOPE_DIM + MLA_V_DIM)
        k_nope, v_mla = kv[..., :MLA_NOPE_DIM], kv[..., MLA_NOPE_DIM:]
        q_pe = apply_rope(q_pe, positions)
        k_pe = apply_rope(k_rope[:, :, None, :], positions)
        q_full = jnp.concatenate([q_nope, q_pe], axis=-1)
        k_full = jnp.concatenate([k_nope, jnp.broadcast_to(k_pe, (B, S, MLA_HEADS, MLA_ROPE_DIM))], axis=-1)
        o_mla = causal_dense_attention(q_full, k_full, v_mla,
                                       (MLA_NOPE_DIM + MLA_ROPE_DIM) ** -0.5).reshape(B, S, MLA_WIDTH)
        y_mla = (o_mla * jax.nn.silu(z_mla)) @ w_o_mla[l]

        qm = apply_rope(q_mb.reshape(B, S, MOBA_HEADS, MOBA_HEAD_DIM), positions)
        km = apply_rope(k_mb.reshape(B, S, MOBA_HEADS, MOBA_HEAD_DIM), positions)
        vm = v_mb.reshape(B, S, MOBA_HEADS, MOBA_HEAD_DIM)
        o_mb = moba_attention(qm, km, vm).reshape(B, S, MOBA_WIDTH)
        y_mb = (o_mb * jax.nn.silu(z_mb)) @ w_o_moba[l]

        gates = jax.nn.sigmoid(merge_logits + b_merge[l])
        g_a, g_b = jnp.split(gates, 2, axis=-1)
        y = (g_a * y_mla + g_b * y_mb) @ w_out[l]

        x = x + gate[:, None, :] * rms_norm(y, g_post[l])
    return x
```

```python
import functools

import jax
import jax.numpy as jnp
from jax import lax
from jax.experimental import pallas as pl
from jax.experimental.pallas import tpu as pltpu

EPS = 1e-6
ROPE_THETA = 10000.0
MLA_HEADS = 8
MLA_NOPE = 64
MLA_ROPE = 32
MLA_V = 64
MLA_Q_RANK = 256
MLA_KV_RANK = 128
MOBA_HEADS = 8
MOBA_DH = 64
MOBA_BLOCK = 256
MOBA_TOPK = 3

LANES = 128
HEAD_SLAB = LANES
ATTN_TILE = MOBA_BLOCK
TOKEN_TILE = 512
ROPE_TILE = 512
VMEM_LIMIT = 56 * 1024 * 1024

F32 = jnp.float32
BF16 = jnp.bfloat16
NEG_INF = float("-inf")


def _sigmoid(t):
    return 1.0 / (1.0 + jnp.exp(-t))


def _silu(t):
    return t * _sigmoid(t)


def _adaln_kernel(c_ref, w_ref, b_ref, o_ref):
    c = c_ref[...]
    o_ref[...] = jnp.dot(_silu(c), w_ref[...], preferred_element_type=F32) + b_ref[...]


def _adaln(c, w_ada, b_ada):
    B, D = c.shape
    n = w_ada.shape[1] // D
    return pl.pallas_call(
        _adaln_kernel,
        grid=(n,),
        in_specs=[pl.BlockSpec((B, D), lambda j: (0, 0)),
                  pl.BlockSpec((D, D), lambda j: (0, j)),
                  pl.BlockSpec((1, D), lambda j: (0, j))],
        out_specs=pl.BlockSpec((B, D), lambda j: (0, j)),
        out_shape=jax.ShapeDtypeStruct((B, n * D), F32),
        compiler_params=pltpu.CompilerParams(vmem_limit_bytes=VMEM_LIMIT),
        name="adaln",
    )(c, w_ada, b_ada.reshape(1, -1))


def _rope_kernel(pos_ref, fmb_ref, fml_ref, cmb_ref, smb_ref, cml_ref, sml_ref):
    nb, ts = pos_ref.shape
    ones = jnp.ones((2 * MLA_ROPE, ts), F32)
    zeros64 = jnp.zeros((2 * MLA_ROPE, ts), F32)
    zeros32 = jnp.zeros((MLA_ROPE, ts), F32)
    for b in range(nb):
        pos = pos_ref[b:b + 1, :].astype(F32)
        amb = fmb_ref[...] * pos
        cb, sb = jnp.cos(amb), jnp.sin(amb)
        cmb_ref[b] = jnp.concatenate([cb, cb, cb, cb], axis=0).T
        smb_ref[b] = jnp.concatenate([-sb, sb, -sb, sb], axis=0).T
        aml = fml_ref[...] * pos
        ca, sa = jnp.cos(aml), jnp.sin(aml)
        cml_ref[b] = jnp.concatenate([ones, ca, ca, zeros32], axis=0).T
        sml_ref[b] = jnp.concatenate([zeros64, -sa, sa, zeros32], axis=0).T


def _rope_tables(positions):
    B, S = positions.shape
    ts = ROPE_TILE
    half_mb, half_ml = MOBA_DH // 2, MLA_ROPE // 2
    fmb = (ROPE_THETA ** (-jnp.arange(half_mb, dtype=F32) / half_mb)).reshape(half_mb, 1)
    fml = (ROPE_THETA ** (-jnp.arange(half_ml, dtype=F32) / half_ml)).reshape(half_ml, 1)
    tab = jax.ShapeDtypeStruct((B, S, LANES), F32)
    tab_spec = pl.BlockSpec((B, ts, LANES), lambda j: (0, j, 0))
    return pl.pallas_call(
        _rope_kernel,
        grid=(S // ts,),
        in_specs=[pl.BlockSpec((B, ts), lambda j: (0, j)),
                  pl.BlockSpec((half_mb, 1), lambda j: (0, 0)),
                  pl.BlockSpec((half_ml, 1), lambda j: (0, 0))],
        out_specs=[tab_spec] * 4,
        out_shape=[tab] * 4,
        compiler_params=pltpu.CompilerParams(vmem_limit_bytes=VMEM_LIMIT),
        name="rope_tables",
    )(positions, fmb, fml)


def _rope_slab(t, cos, sin, half):
    lane = lax.broadcasted_iota(jnp.int32, t.shape, 1)
    first = (lane & (2 * half - 1)) < half
    partner = jnp.where(first, pltpu.roll(t, LANES - half, 1), pltpu.roll(t, half, 1))
    return t * cos + partner * sin


def _rms(t, g):
    return t * lax.rsqrt(jnp.mean(t * t, axis=-1, keepdims=True) + EPS) * g


def _inproj_kernel(x_ref, mod_ref, gpre_ref, w_ref, gq_ref, wuq_ref, gkv_ref, wuk_ref, wuv_ref,
                   bm_ref, cmb_ref, smb_ref, cml_ref, sml_ref,
                   qml_ref, kml_ref, vtml_ref, zml_ref, qmb_ref, kmb_ref, vtmb_ref, zmb_ref,
                   ga_ref, gb_ref, kmean_ref):
    tm = x_ref.shape[1]
    nsub = tm // ATTN_TILE
    x = x_ref[0]
    shift = mod_ref[0, 0:1, :]
    scale = mod_ref[0, 1:2, :]
    h = (_rms(x, gpre_ref[...]) * (1.0 + scale) + shift).astype(BF16)

    def proj(lo, hi):
        return jnp.dot(h, w_ref[:, lo:hi], preferred_element_type=F32)

    cmb, smb = cmb_ref[0], smb_ref[0]
    cml, sml = cml_ref[0], sml_ref[0]

    g0 = proj(0, 512)
    q_scale = float((MLA_NOPE + MLA_ROPE) ** -0.5)
    qn = _rms(g0[:, :MLA_Q_RANK], gq_ref[...] * q_scale).astype(BF16)
    kvn = _rms(g0[:, MLA_Q_RANK:MLA_Q_RANK + MLA_KV_RANK], gkv_ref[...]).astype(BF16)
    kpe = _rope_slab(g0[:, 384:512], cml, sml, MLA_ROPE // 2)
    q = jnp.dot(qn, wuq_ref[...], preferred_element_type=F32)
    kn = jnp.dot(kvn, wuk_ref[...], preferred_element_type=F32)
    for hd in range(MLA_HEADS):
        sl = slice(hd * HEAD_SLAB, (hd + 1) * HEAD_SLAB)
        qml_ref[0, :, sl] = _rope_slab(q[:, sl], cml, sml, MLA_ROPE // 2).astype(BF16)
        kml_ref[0, :, sl] = (kn[:, sl] + kpe).astype(BF16)
    v = jnp.dot(kvn, wuv_ref[...], preferred_element_type=F32)
    for r in range(nsub):
        vtml_ref[0, r] = v[r * ATTN_TILE:(r + 1) * ATTN_TILE, :].T.astype(BF16)
    zml_ref[0] = _silu(proj(512, 1024)).astype(BF16)

    qm = proj(1024, 1536)
    km = proj(1536, 2048)
    mb_scale = float(MOBA_DH ** -0.5)
    for cidx in range(MOBA_HEADS * MOBA_DH // LANES):
        sl = slice(cidx * LANES, (cidx + 1) * LANES)
        qmb_ref[0, :, sl] = (_rope_slab(qm[:, sl], cmb, smb, MOBA_DH // 2) * mb_scale).astype(BF16)
        kr = _rope_slab(km[:, sl], cmb, smb, MOBA_DH // 2)
        kmb_ref[0, :, sl] = kr.astype(BF16)
        for r in range(nsub):
            kmean_ref[0, r, :, sl] = jnp.mean(kr[r * ATTN_TILE:(r + 1) * ATTN_TILE, :], axis=0, keepdims=True)
    vm = proj(2048, 2560)
    for r in range(nsub):
        vtmb_ref[0, r] = vm[r * ATTN_TILE:(r + 1) * ATTN_TILE, :].T.astype(BF16)
    zmb_ref[0] = _silu(proj(2560, 3072)).astype(BF16)

    D = x_ref.shape[2]
    ga_ref[0] = _sigmoid(proj(3072, 3072 + D) + bm_ref[:, :D]).astype(BF16)
    gb_ref[0] = _sigmoid(proj(3072 + D, 3072 + 2 * D) + bm_ref[:, D:]).astype(BF16)


def _prep_in_weights(w_in, w_uq, w_ukv):
    D = w_in.shape[0]
    o = 0
    cols = {}
    for name, n in (("q_lat", MLA_Q_RANK), ("kv_lat", MLA_KV_RANK), ("k_rope", MLA_ROPE),
                    ("z_mla", 512), ("q_mb", 512), ("k_mb", 512), ("v_mb", 512), ("z_mb", 512),
                    ("merge", 2 * D)):
        cols[name] = w_in[:, o:o + n]
        o += n
    zpad = lambda n: jnp.zeros((D, n), w_in.dtype)
    w = jnp.concatenate([cols["q_lat"], cols["kv_lat"],
                         zpad(MLA_NOPE), cols["k_rope"], zpad(HEAD_SLAB - MLA_NOPE - MLA_ROPE),
                         cols["z_mla"], cols["q_mb"], cols["k_mb"], cols["v_mb"], cols["z_mb"],
                         cols["merge"]], axis=1).astype(BF16)
    dq = MLA_NOPE + MLA_ROPE
    wuq = w_uq.reshape(MLA_Q_RANK, MLA_HEADS, dq)
    wuq = jnp.pad(wuq, ((0, 0), (0, 0), (0, HEAD_SLAB - dq))).reshape(MLA_Q_RANK, MLA_HEADS * HEAD_SLAB)
    wukv = w_ukv.reshape(MLA_KV_RANK, MLA_HEADS, MLA_NOPE + MLA_V)
    wuk = jnp.pad(wukv[:, :, :MLA_NOPE], ((0, 0), (0, 0), (0, HEAD_SLAB - MLA_NOPE)))
    wuk = wuk.reshape(MLA_KV_RANK, MLA_HEADS * HEAD_SLAB)
    wuv = wukv[:, :, MLA_NOPE:].reshape(MLA_KV_RANK, MLA_HEADS * MLA_V)
    return w, wuq.astype(BF16), wuk.astype(BF16), wuv.astype(BF16)


def _inproj(x, mod, g_pre, w, g_q, wuq, g_kv, wuk, wuv, b_merge, tabs):
    B, S, D = x.shape
    tm = TOKEN_TILE
    nkb = S // ATTN_TILE
    nsub = tm // ATTN_TILE
    wide = MLA_HEADS * HEAD_SLAB
    mbw = MOBA_HEADS * MOBA_DH

    def const(shape):
        return pl.BlockSpec(shape, lambda b, i: (0,) * len(shape), pipeline_mode=pl.Buffered(1))

    def rows(n):
        return pl.BlockSpec((1, tm, n), lambda b, i: (b, i, 0))

    vt_spec = pl.BlockSpec((1, nsub, mbw, ATTN_TILE), lambda b, i: (b, i, 0, 0))
    bf = lambda *s: jax.ShapeDtypeStruct(s, BF16)
    out_shape = [bf(B, S, wide), bf(B, S, wide), bf(B, nkb, mbw, ATTN_TILE), bf(B, S, mbw),
                 bf(B, S, mbw), bf(B, S, mbw), bf(B, nkb, mbw, ATTN_TILE), bf(B, S, mbw),
                 bf(B, S, D), bf(B, S, D),
                 jax.ShapeDtypeStruct((B, nkb, 1, mbw), F32)]
    out_specs = [rows(wide), rows(wide), vt_spec, rows(mbw),
                 rows(mbw), rows(mbw), vt_spec, rows(mbw),
                 rows(D), rows(D),
                 pl.BlockSpec((1, nsub, 1, mbw), lambda b, i: (b, i, 0, 0))]
    in_specs = [rows(D),
                pl.BlockSpec((1, 3, D), lambda b, i: (b, 0, 0)),
                const((1, D)), const(w.shape), const((1, MLA_Q_RANK)), const(wuq.shape),
                const((1, MLA_KV_RANK)), const(wuk.shape), const(wuv.shape), const((1, 2 * D)),
                rows(LANES), rows(LANES), rows(LANES), rows(LANES)]
    return pl.pallas_call(
        _inproj_kernel,
        grid=(B, S // tm),
        in_specs=in_specs,
        out_specs=out_specs,
        out_shape=out_shape,
        compiler_params=pltpu.CompilerParams(
            dimension_semantics=("parallel", "parallel"), vmem_limit_bytes=VMEM_LIMIT),
        name="inproj",
    )(x, mod, g_pre, w, g_q, wuq, g_kv, wuk, wuv, b_merge, *tabs)


def _attn_kernel(*refs, moba):
    if moba:
        q_ref, k_ref, vt_ref, kmean_ref, o_ref, bias_ref = refs
    else:
        q_ref, k_ref, vt_ref, o_ref = refs
    t = ATTN_TILE
    i = pl.program_id(2)
    dv = vt_ref.shape[2] // 2
    row = lax.broadcasted_iota(jnp.int32, (t, t), 0)
    col = lax.broadcasted_iota(jnp.int32, (t, t), 1)
    causal = row <= col

    if moba:
        q2 = q_ref[0]
        lane = lax.broadcasted_iota(jnp.int32, q2.shape, 1)
        qs = [jnp.where(lane < MOBA_DH, q2, jnp.zeros_like(q2)),
              jnp.where(lane >= MOBA_DH, q2, jnp.zeros_like(q2))]
        nkb = kmean_ref.shape[1]
        blk = lax.broadcasted_iota(jnp.int32, (nkb, t), 0).astype(F32)
        for hd in range(2):
            g = lax.dot_general(kmean_ref[0], qs[hd].astype(F32), (((1,), (1,)), ((), ())),
                                preferred_element_type=F32)
            g = jnp.where(blk < i.astype(F32), g, NEG_INF)
            bias = jnp.full((nkb, t), NEG_INF, F32)
            for _ in range(MOBA_TOPK):
                best = jnp.max(g, axis=0, keepdims=True)
                hit = (g == best) & (best > NEG_INF)
                first = jnp.min(jnp.where(hit, blk, float(nkb)), axis=0, keepdims=True)
                pick = blk == first
                bias = jnp.where(pick, 0.0, bias)
                g = jnp.where(pick, NEG_INF, g)
            bias_ref[hd] = bias
    else:
        qs = [q_ref[0, :, hd * HEAD_SLAB:(hd + 1) * HEAD_SLAB] for hd in range(2)]

    def k_block(j, hd):
        start = pl.multiple_of(j * t, t)
        if moba:
            return k_ref[0, pl.ds(start, t), :]
        return k_ref[0, pl.ds(start, t), hd * HEAD_SLAB:(hd + 1) * HEAD_SLAB]

    def scores(j, hd):
        return lax.dot_general(k_block(j, hd), qs[hd], (((1,), (1,)), ((), ())),
                               preferred_element_type=F32)

    def pv(j, hd, p):
        vt = vt_ref[0, j, hd * dv:(hd + 1) * dv, :]
        return jnp.dot(vt, p.astype(BF16), preferred_element_type=F32)

    state = []
    for hd in range(2):
        s = jnp.where(causal, scores(i, hd), NEG_INF)
        m = jnp.max(s, axis=0, keepdims=True)
        p = jnp.exp(s - m)
        state += [m, jnp.sum(p, axis=0, keepdims=True), pv(i, hd, p)]

    def body(j, carry):
        out = []
        for hd in range(2):
            m, l, acc = carry[3 * hd:3 * hd + 3]
            s = scores(j, hd)
            if moba:
                s = s + bias_ref[hd, pl.ds(j, 1), :]
            m_new = jnp.maximum(m, jnp.max(s, axis=0, keepdims=True))
            alpha = jnp.exp(m - m_new)
            p = jnp.exp(s - m_new)
            l = alpha * l + jnp.sum(p, axis=0, keepdims=True)
            acc = alpha * acc + pv(j, hd, p)
            out += [m_new, l, acc]
        return tuple(out)

    state = lax.fori_loop(0, i, body, tuple(state))
    o_t = jnp.concatenate([state[2] * (1.0 / state[1]), state[5] * (1.0 / state[4])], axis=0)
    o_ref[0] = o_t.T.astype(o_ref.dtype)


def _attention(q, k, vt, kmean=None):
    B, S, _ = q.shape
    t = ATTN_TILE
    nkb = S // t
    pairs = vt.shape[2] // LANES
    wq = q.shape[2] // pairs
    moba = kmean is not None
    in_specs = [pl.BlockSpec((1, t, wq), lambda b, p, i: (b, i, p)),
                pl.BlockSpec((1, S, wq), lambda b, p, i: (b, 0, p)),
                pl.BlockSpec((1, nkb, LANES, t), lambda b, p, i: (b, 0, p, 0))]
    args = [q, k, vt]
    scratch = []
    if moba:
        in_specs.append(pl.BlockSpec((1, nkb, LANES), lambda b, p, i: (b, 0, p)))
        args.append(kmean)
        scratch.append(pltpu.VMEM((2, nkb, t), F32))
    return pl.pallas_call(
        functools.partial(_attn_kernel, moba=moba),
        grid=(B, pairs, nkb),
        in_specs=in_specs,
        out_specs=pl.BlockSpec((1, t, LANES), lambda b, p, i: (b, i, p)),
        out_shape=jax.ShapeDtypeStruct((B, S, pairs * LANES), BF16),
        scratch_shapes=scratch,
        compiler_params=pltpu.CompilerParams(
            dimension_semantics=("parallel", "parallel", "parallel"), vmem_limit_bytes=VMEM_LIMIT),
        name="moba_attn" if moba else "mla_attn",
    )(*args)


def _outproj_kernel(x_ref, mod_ref, oml_ref, zml_ref, omb_ref, zmb_ref, ga_ref, gb_ref,
                    woa_ref, wob_ref, wout_ref, gpost_ref, o_ref):
    y_a = jnp.dot(oml_ref[0] * zml_ref[0], woa_ref[...], preferred_element_type=F32)
    y_b = jnp.dot(omb_ref[0] * zmb_ref[0], wob_ref[...], preferred_element_type=F32)
    u = ga_ref[0].astype(F32) * y_a + gb_ref[0].astype(F32) * y_b
    y = jnp.dot(u.astype(BF16), wout_ref[...], preferred_element_type=F32)
    gate = mod_ref[0, 2:3, :]
    o_ref[0] = x_ref[0] + gate * _rms(y, gpost_ref[...])


def _outproj(x, mod, o_mla, z_mla, o_mb, z_mb, g_a, g_b, woa, wob, wout, g_post):
    B, S, D = x.shape
    tm = TOKEN_TILE
    width = o_mla.shape[2]

    def const(shape):
        return pl.BlockSpec(shape, lambda b, i: (0,) * len(shape), pipeline_mode=pl.Buffered(1))

    def rows(n):
        return pl.BlockSpec((1, tm, n), lambda b, i: (b, i, 0))

    return pl.pallas_call(
        _outproj_kernel,
        grid=(B, S // tm),
        in_specs=[rows(D), pl.BlockSpec((1, 3, D), lambda b, i: (b, 0, 0)),
                  rows(width), rows(width), rows(width), rows(width), rows(D), rows(D),
                  const(woa.shape), const(wob.shape), const(wout.shape), const((1, D))],
        out_specs=rows(D),
        out_shape=jax.ShapeDtypeStruct((B, S, D), x.dtype),
        compiler_params=pltpu.CompilerParams(
            dimension_semantics=("parallel", "parallel"), vmem_limit_bytes=VMEM_LIMIT),
        name="outproj",
    )(x, mod, o_mla, z_mla, o_mb, z_mb, g_a, g_b, woa, wob, wout, g_post)


def kernel(x, c, positions, w_ada, b_ada, g_pre, g_post, w_in, g_q_lat, w_uq, g_kv_lat, w_ukv,
           w_o_mla, w_o_moba, b_merge, w_out):
    B, S, D = x.shape
    depth = w_ada.shape[0]
    assert S % TOKEN_TILE == 0 and S % ROPE_TILE == 0 and TOKEN_TILE % ATTN_TILE == 0
    assert w_in.shape[2] == (MLA_Q_RANK + MLA_KV_RANK + MLA_ROPE + 5 * 512 + 2 * D)
    tabs = _rope_tables(positions)
    for l in range(depth):
        mod = _adaln(c, w_ada[l], b_ada[l]).reshape(B, 3, D)
        w, wuq, wuk, wuv = _prep_in_weights(w_in[l], w_uq[l], w_ukv[l])
        (q_ml, k_ml, vt_ml, z_ml, q_mb, k_mb, vt_mb, z_mb, g_a, g_b, kmean) = _inproj(
            x, mod, g_pre[l][None], w, g_q_lat[l][None], wuq, g_kv_lat[l][None], wuk, wuv,
            b_merge[l][None], tabs)
        o_ml = _attention(q_ml, k_ml, vt_ml)
        o_mb = _attention(q_mb, k_mb, vt_mb, kmean.reshape(B, S // ATTN_TILE, -1))
        x = _outproj(x, mod, o_ml, z_ml, o_mb, z_mb, g_a, g_b,
                     w_o_mla[l].astype(BF16), w_o_moba[l].astype(BF16), w_out[l].astype(BF16),
                     g_post[l][None])
    return x
```

```python
import functools

import jax
import jax.numpy as jnp
from jax import lax
from jax.experimental import pallas as pl
from jax.experimental.pallas import tpu as pltpu

EPS = 1e-6
ROPE_THETA = 10000.0
MLA_HEADS = 8
MLA_NOPE = 64
MLA_ROPE = 32
MLA_V = 64
MLA_Q_RANK = 256
MLA_KV_RANK = 128
MOBA_HEADS = 8
MOBA_DH = 64
MOBA_BLOCK = 256
MOBA_TOPK = 3

LANES = 128
HEAD_SLAB = LANES
ATTN_TILE = MOBA_BLOCK
KV_GROUP = 4
MASK_FLOOR = -1e30
SUM_ROWS = 16
LOG2E = 1.4426950408889634
TOKEN_TILE = 512
ROPE_TILE = 512
VMEM_LIMIT = 56 * 1024 * 1024

F32 = jnp.float32
BF16 = jnp.bfloat16
NEG_INF = float("-inf")


def _sigmoid(t):
    return 1.0 / (1.0 + jnp.exp(-t))


def _silu(t):
    return t * _sigmoid(t)


def _adaln_kernel(c_ref, w_ref, b_ref, o_ref):
    c = c_ref[...]
    o_ref[...] = jnp.dot(_silu(c), w_ref[...], preferred_element_type=F32) + b_ref[...]


def _adaln(c, w_ada, b_ada):
    B, D = c.shape
    n = w_ada.shape[1] // D
    return pl.pallas_call(
        _adaln_kernel,
        grid=(n,),
        in_specs=[pl.BlockSpec((B, D), lambda j: (0, 0)),
                  pl.BlockSpec((D, D), lambda j: (0, j)),
                  pl.BlockSpec((1, D), lambda j: (0, j))],
        out_specs=pl.BlockSpec((B, D), lambda j: (0, j)),
        out_shape=jax.ShapeDtypeStruct((B, n * D), F32),
        compiler_params=pltpu.CompilerParams(vmem_limit_bytes=VMEM_LIMIT),
        name="adaln",
    )(c, w_ada, b_ada.reshape(1, -1))


def _rope_kernel(pos_ref, fmb_ref, fml_ref, cmb_ref, smb_ref, cml_ref, sml_ref):
    nb, ts = pos_ref.shape
    ones = jnp.ones((2 * MLA_ROPE, ts), F32)
    zeros64 = jnp.zeros((2 * MLA_ROPE, ts), F32)
    zeros32 = jnp.zeros((MLA_ROPE, ts), F32)
    for b in range(nb):
        pos = pos_ref[b:b + 1, :].astype(F32)
        amb = fmb_ref[...] * pos
        cb, sb = jnp.cos(amb), jnp.sin(amb)
        cmb_ref[b] = jnp.concatenate([cb, cb, cb, cb], axis=0).T
        smb_ref[b] = jnp.concatenate([-sb, sb, -sb, sb], axis=0).T
        aml = fml_ref[...] * pos
        ca, sa = jnp.cos(aml), jnp.sin(aml)
        cml_ref[b] = jnp.concatenate([ones, ca, ca, zeros32], axis=0).T
        sml_ref[b] = jnp.concatenate([zeros64, -sa, sa, zeros32], axis=0).T


def _rope_tables(positions):
    B, S = positions.shape
    ts = ROPE_TILE
    half_mb, half_ml = MOBA_DH // 2, MLA_ROPE // 2
    fmb = (ROPE_THETA ** (-jnp.arange(half_mb, dtype=F32) / half_mb)).reshape(half_mb, 1)
    fml = (ROPE_THETA ** (-jnp.arange(half_ml, dtype=F32) / half_ml)).reshape(half_ml, 1)
    tab = jax.ShapeDtypeStruct((B, S, LANES), F32)
    tab_spec = pl.BlockSpec((B, ts, LANES), lambda j: (0, j, 0))
    return pl.pallas_call(
        _rope_kernel,
        grid=(S // ts,),
        in_specs=[pl.BlockSpec((B, ts), lambda j: (0, j)),
                  pl.BlockSpec((half_mb, 1), lambda j: (0, 0)),
                  pl.BlockSpec((half_ml, 1), lambda j: (0, 0))],
        out_specs=[tab_spec] * 4,
        out_shape=[tab] * 4,
        compiler_params=pltpu.CompilerParams(vmem_limit_bytes=VMEM_LIMIT),
        name="rope_tables",
    )(positions, fmb, fml)


def _rope_slab(t, cos, sin, half):
    lane = lax.broadcasted_iota(jnp.int32, t.shape, 1)
    first = (lane & (2 * half - 1)) < half
    partner = jnp.where(first, pltpu.roll(t, LANES - half, 1), pltpu.roll(t, half, 1))
    return t * cos + partner * sin


def _rms(t, g):
    return t * lax.rsqrt(jnp.mean(t * t, axis=-1, keepdims=True) + EPS) * g


def _inproj_kernel(x_ref, mod_ref, gpre_ref, w_ref, gq_ref, wuq_ref, gkv_ref, wuk_ref, wuv_ref,
                   bm_ref, cmb_ref, smb_ref, cml_ref, sml_ref,
                   qml_ref, kml_ref, vtml_ref, zml_ref, qmb_ref, kmb_ref, vtmb_ref, zmb_ref,
                   ga_ref, gb_ref, kmean_ref):
    tm = x_ref.shape[1]
    nsub = tm // ATTN_TILE
    x = x_ref[0]
    shift = mod_ref[0, 0:1, :]
    scale = mod_ref[0, 1:2, :]
    h = (_rms(x, gpre_ref[...]) * (1.0 + scale) + shift).astype(BF16)

    def proj(lo, hi):
        return jnp.dot(h, w_ref[:, lo:hi], preferred_element_type=F32)

    cmb, smb = cmb_ref[0], smb_ref[0]
    cml, sml = cml_ref[0], sml_ref[0]

    g0 = proj(0, 512)
    q_scale = float((MLA_NOPE + MLA_ROPE) ** -0.5) * LOG2E
    qn = _rms(g0[:, :MLA_Q_RANK], gq_ref[...] * q_scale).astype(BF16)
    kvn = _rms(g0[:, MLA_Q_RANK:MLA_Q_RANK + MLA_KV_RANK], gkv_ref[...]).astype(BF16)
    kpe = _rope_slab(g0[:, 384:512], cml, sml, MLA_ROPE // 2)
    q = jnp.dot(qn, wuq_ref[...], preferred_element_type=F32)
    kn = jnp.dot(kvn, wuk_ref[...], preferred_element_type=F32)
    for hd in range(MLA_HEADS):
        sl = slice(hd * HEAD_SLAB, (hd + 1) * HEAD_SLAB)
        qml_ref[0, :, sl] = _rope_slab(q[:, sl], cml, sml, MLA_ROPE // 2).astype(BF16)
        kml_ref[0, :, sl] = (kn[:, sl] + kpe).astype(BF16)
    v = jnp.dot(kvn, wuv_ref[...], preferred_element_type=F32)
    for r in range(nsub):
        vtml_ref[0, r] = v[r * ATTN_TILE:(r + 1) * ATTN_TILE, :].T.astype(BF16)
    zml_ref[0] = _silu(proj(512, 1024)).astype(BF16)

    qm = proj(1024, 1536)
    km = proj(1536, 2048)
    mb_scale = float(MOBA_DH ** -0.5) * LOG2E
    for cidx in range(MOBA_HEADS * MOBA_DH // LANES):
        sl = slice(cidx * LANES, (cidx + 1) * LANES)
        qmb_ref[0, :, sl] = (_rope_slab(qm[:, sl], cmb, smb, MOBA_DH // 2) * mb_scale).astype(BF16)
        kr = _rope_slab(km[:, sl], cmb, smb, MOBA_DH // 2)
        kmb_ref[0, :, sl] = kr.astype(BF16)
        for r in range(nsub):
            kmean_ref[0, r, :, sl] = jnp.mean(kr[r * ATTN_TILE:(r + 1) * ATTN_TILE, :], axis=0, keepdims=True)
    vm = proj(2048, 2560)
    for r in range(nsub):
        vtmb_ref[0, r] = vm[r * ATTN_TILE:(r + 1) * ATTN_TILE, :].T.astype(BF16)
    zmb_ref[0] = _silu(proj(2560, 3072)).astype(BF16)

    D = x_ref.shape[2]
    ga_ref[0] = _sigmoid(proj(3072, 3072 + D) + bm_ref[:, :D]).astype(BF16)
    gb_ref[0] = _sigmoid(proj(3072 + D, 3072 + 2 * D) + bm_ref[:, D:]).astype(BF16)


def _prep_in_weights(w_in, w_uq, w_ukv):
    D = w_in.shape[0]
    o = 0
    cols = {}
    for name, n in (("q_lat", MLA_Q_RANK), ("kv_lat", MLA_KV_RANK), ("k_rope", MLA_ROPE),
                    ("z_mla", 512), ("q_mb", 512), ("k_mb", 512), ("v_mb", 512), ("z_mb", 512),
                    ("merge", 2 * D)):
        cols[name] = w_in[:, o:o + n]
        o += n
    zpad = lambda n: jnp.zeros((D, n), w_in.dtype)
    w = jnp.concatenate([cols["q_lat"], cols["kv_lat"],
                         zpad(MLA_NOPE), cols["k_rope"], zpad(HEAD_SLAB - MLA_NOPE - MLA_ROPE),
                         cols["z_mla"], cols["q_mb"], cols["k_mb"], cols["v_mb"], cols["z_mb"],
                         cols["merge"]], axis=1).astype(BF16)
    dq = MLA_NOPE + MLA_ROPE
    wuq = w_uq.reshape(MLA_Q_RANK, MLA_HEADS, dq)
    wuq = jnp.pad(wuq, ((0, 0), (0, 0), (0, HEAD_SLAB - dq))).reshape(MLA_Q_RANK, MLA_HEADS * HEAD_SLAB)
    wukv = w_ukv.reshape(MLA_KV_RANK, MLA_HEADS, MLA_NOPE + MLA_V)
    wuk = jnp.pad(wukv[:, :, :MLA_NOPE], ((0, 0), (0, 0), (0, HEAD_SLAB - MLA_NOPE)))
    wuk = wuk.reshape(MLA_KV_RANK, MLA_HEADS * HEAD_SLAB)
    wuv = wukv[:, :, MLA_NOPE:].reshape(MLA_KV_RANK, MLA_HEADS * MLA_V)
    return w, wuq.astype(BF16), wuk.astype(BF16), wuv.astype(BF16)


def _inproj(x, mod, g_pre, w, g_q, wuq, g_kv, wuk, wuv, b_merge, tabs):
    B, S, D = x.shape
    tm = TOKEN_TILE
    nkb = S // ATTN_TILE
    nsub = tm // ATTN_TILE
    wide = MLA_HEADS * HEAD_SLAB
    mbw = MOBA_HEADS * MOBA_DH

    def const(shape):
        return pl.BlockSpec(shape, lambda b, i: (0,) * len(shape), pipeline_mode=pl.Buffered(1))

    def rows(n):
        return pl.BlockSpec((1, tm, n), lambda b, i: (b, i, 0))

    vt_spec = pl.BlockSpec((1, nsub, mbw, ATTN_TILE), lambda b, i: (b, i, 0, 0))
    bf = lambda *s: jax.ShapeDtypeStruct(s, BF16)
    out_shape = [bf(B, S, wide), bf(B, S, wide), bf(B, nkb, mbw, ATTN_TILE), bf(B, S, mbw),
                 bf(B, S, mbw), bf(B, S, mbw), bf(B, nkb, mbw, ATTN_TILE), bf(B, S, mbw),
                 bf(B, S, D), bf(B, S, D),
                 jax.ShapeDtypeStruct((B, nkb, 1, mbw), F32)]
    out_specs = [rows(wide), rows(wide), vt_spec, rows(mbw),
                 rows(mbw), rows(mbw), vt_spec, rows(mbw),
                 rows(D), rows(D),
                 pl.BlockSpec((1, nsub, 1, mbw), lambda b, i: (b, i, 0, 0))]
    in_specs = [rows(D),
                pl.BlockSpec((1, 3, D), lambda b, i: (b, 0, 0)),
                const((1, D)), const(w.shape), const((1, MLA_Q_RANK)), const(wuq.shape),
                const((1, MLA_KV_RANK)), const(wuk.shape), const(wuv.shape), const((1, 2 * D)),
                rows(LANES), rows(LANES), rows(LANES), rows(LANES)]
    return pl.pallas_call(
        _inproj_kernel,
        grid=(B, S // tm),
        in_specs=in_specs,
        out_specs=out_specs,
        out_shape=out_shape,
        compiler_params=pltpu.CompilerParams(
            dimension_semantics=("parallel", "parallel"), vmem_limit_bytes=VMEM_LIMIT),
        name="inproj",
    )(x, mod, g_pre, w, g_q, wuq, g_kv, wuk, wuv, b_merge, *tabs)


def _attn_kernel(*refs, moba):
    if moba:
        q_ref, k_ref, vt_ref, kmean_ref, o_ref, bias_ref = refs
    else:
        q_ref, k_ref, vt_ref, o_ref = refs
    t = ATTN_TILE
    G = KV_GROUP
    i = pl.program_id(2)
    dv = vt_ref.shape[2] // 2
    row = lax.broadcasted_iota(jnp.int32, (t, t), 0)
    col = lax.broadcasted_iota(jnp.int32, (t, t), 1)
    key_minus_query = row - col

    if moba:
        q2 = q_ref[0]
        lane = lax.broadcasted_iota(jnp.int32, q2.shape, 1)
        qs = [jnp.where(lane < MOBA_DH, q2, jnp.zeros_like(q2)),
              jnp.where(lane >= MOBA_DH, q2, jnp.zeros_like(q2))]
        nkb = kmean_ref.shape[1]
        blk = lax.broadcasted_iota(jnp.int32, (nkb, t), 0).astype(F32)
        for hd in range(2):
            g = lax.dot_general(kmean_ref[0], qs[hd].astype(F32), (((1,), (1,)), ((), ())),
                                preferred_element_type=F32)
            g = jnp.where(blk < i.astype(F32), g, NEG_INF)
            bias = jnp.full((nkb, t), NEG_INF, F32)
            for _ in range(MOBA_TOPK):
                best = jnp.max(g, axis=0, keepdims=True)
                hit = (g == best) & (best > NEG_INF)
                first = jnp.min(jnp.where(hit, blk, float(nkb)), axis=0, keepdims=True)
                pick = blk == first
                bias = jnp.where(pick, 0.0, bias)
                g = jnp.where(pick, NEG_INF, g)
            bias_ref[hd] = jnp.where(blk == i.astype(F32), 0.0, bias)
    else:
        qs = [q_ref[0, :, hd * HEAD_SLAB:(hd + 1) * HEAD_SLAB] for hd in range(2)]

    def k_block(j, hd):
        start = pl.multiple_of(j * t, t)
        if moba:
            return k_ref[0, pl.ds(start, t), :]
        return k_ref[0, pl.ds(start, t), hd * HEAD_SLAB:(hd + 1) * HEAD_SLAB]

    def scores(j, hd):
        return lax.dot_general(k_block(j, hd), qs[hd], (((1,), (1,)), ((), ())),
                               preferred_element_type=F32)

    ones_rows = jnp.ones((SUM_ROWS, t), BF16)

    def pv(j, hd, p):
        vt = jnp.concatenate([vt_ref[0, j, hd * dv:(hd + 1) * dv, :], ones_rows], axis=0)
        return jnp.dot(vt, p.astype(BF16), preferred_element_type=F32)

    def group(grp, carry, masked):
        all_ss = []
        for hd in range(2):
            ss = []
            for r in range(G):
                j = grp * G + r
                s = scores(j, hd)
                if moba:
                    s = s + bias_ref[hd, pl.ds(j, 1), :]
                if masked:
                    s = jnp.where(key_minus_query <= (i - j) * t, s, NEG_INF)
                ss.append(s)
            all_ss.append(ss)
        out = []
        for hd in range(2):
            m, acc = carry[2 * hd:2 * hd + 2]
            m_new = m
            for s in all_ss[hd]:
                m_new = jnp.maximum(m_new, jnp.max(s, axis=0, keepdims=True))
            acc = jnp.exp2(m - m_new) * acc
            for r, s in enumerate(all_ss[hd]):
                acc = acc + pv(grp * G + r, hd, jnp.exp2(s - m_new))
            out += [m_new, acc]
        return tuple(out)

    init = (jnp.full((1, t), MASK_FLOOR, F32), jnp.zeros((dv + SUM_ROWS, t), F32))
    n_full = i // G
    state = lax.fori_loop(0, n_full, functools.partial(group, masked=False), init + init)
    state = group(n_full, state, masked=True)
    o_t = jnp.concatenate([state[1][:dv] * (1.0 / state[1][dv:dv + 1]),
                           state[3][:dv] * (1.0 / state[3][dv:dv + 1])], axis=0)
    o_ref[0] = o_t.T.astype(o_ref.dtype)


def _attention(q, k, vt, kmean=None):
    B, S, _ = q.shape
    t = ATTN_TILE
    nkb = S // t
    pairs = vt.shape[2] // LANES
    wq = q.shape[2] // pairs
    moba = kmean is not None
    in_specs = [pl.BlockSpec((1, t, wq), lambda b, p, i: (b, i, p)),
                pl.BlockSpec((1, S, wq), lambda b, p, i: (b, 0, p)),
                pl.BlockSpec((1, nkb, LANES, t), lambda b, p, i: (b, 0, p, 0))]
    args = [q, k, vt]
    scratch = []
    if moba:
        in_specs.append(pl.BlockSpec((1, nkb, LANES), lambda b, p, i: (b, 0, p)))
        args.append(kmean)
        scratch.append(pltpu.VMEM((2, nkb, t), F32))
    return pl.pallas_call(
        functools.partial(_attn_kernel, moba=moba),
        grid=(B, pairs, nkb),
        in_specs=in_specs,
        out_specs=pl.BlockSpec((1, t, LANES), lambda b, p, i: (b, i, p)),
        out_shape=jax.ShapeDtypeStruct((B, S, pairs * LANES), BF16),
        scratch_shapes=scratch,
        compiler_params=pltpu.CompilerParams(
            dimension_semantics=("parallel", "parallel", "parallel"), vmem_limit_bytes=VMEM_LIMIT),
        name="moba_attn" if moba else "mla_attn",
    )(*args)


def _outproj_kernel(x_ref, mod_ref, oml_ref, zml_ref, omb_ref, zmb_ref, ga_ref, gb_ref,
                    woa_ref, wob_ref, wout_ref, gpost_ref, o_ref):
    y_a = jnp.dot(oml_ref[0] * zml_ref[0], woa_ref[...], preferred_element_type=F32)
    y_b = jnp.dot(omb_ref[0] * zmb_ref[0], wob_ref[...], preferred_element_type=F32)
    u = ga_ref[0].astype(F32) * y_a + gb_ref[0].astype(F32) * y_b
    y = jnp.dot(u.astype(BF16), wout_ref[...], preferred_element_type=F32)
    gate = mod_ref[0, 2:3, :]
    o_ref[0] = x_ref[0] + gate * _rms(y, gpost_ref[...])


def _outproj(x, mod, o_mla, z_mla, o_mb, z_mb, g_a, g_b, woa, wob, wout, g_post):
    B, S, D = x.shape
    tm = TOKEN_TILE
    width = o_mla.shape[2]

    def const(shape):
        return pl.BlockSpec(shape, lambda b, i: (0,) * len(shape), pipeline_mode=pl.Buffered(1))

    def rows(n):
        return pl.BlockSpec((1, tm, n), lambda b, i: (b, i, 0))

    return pl.pallas_call(
        _outproj_kernel,
        grid=(B, S // tm),
        in_specs=[rows(D), pl.BlockSpec((1, 3, D), lambda b, i: (b, 0, 0)),
                  rows(width), rows(width), rows(width), rows(width), rows(D), rows(D),
                  const(woa.shape), const(wob.shape), const(wout.shape), const((1, D))],
        out_specs=rows(D),
        out_shape=jax.ShapeDtypeStruct((B, S, D), x.dtype),
        compiler_params=pltpu.CompilerParams(
            dimension_semantics=("parallel", "parallel"), vmem_limit_bytes=VMEM_LIMIT),
        name="outproj",
    )(x, mod, o_mla, z_mla, o_mb, z_mb, g_a, g_b, woa, wob, wout, g_post)


def kernel(x, c, positions, w_ada, b_ada, g_pre, g_post, w_in, g_q_lat, w_uq, g_kv_lat, w_ukv,
           w_o_mla, w_o_moba, b_merge, w_out):
    B, S, D = x.shape
    depth = w_ada.shape[0]
    assert S % TOKEN_TILE == 0 and S % ROPE_TILE == 0 and TOKEN_TILE % ATTN_TILE == 0
    assert w_in.shape[2] == (MLA_Q_RANK + MLA_KV_RANK + MLA_ROPE + 5 * 512 + 2 * D)
    tabs = _rope_tables(positions)
    for l in range(depth):
        mod = _adaln(c, w_ada[l], b_ada[l]).reshape(B, 3, D)
        w, wuq, wuk, wuv = _prep_in_weights(w_in[l], w_uq[l], w_ukv[l])
        (q_ml, k_ml, vt_ml, z_ml, q_mb, k_mb, vt_mb, z_mb, g_a, g_b, kmean) = _inproj(
            x, mod, g_pre[l][None], w, g_q_lat[l][None], wuq, g_kv_lat[l][None], wuk, wuv,
            b_merge[l][None], tabs)
        o_ml = _attention(q_ml, k_ml, vt_ml)
        o_mb = _attention(q_mb, k_mb, vt_mb, kmean.reshape(B, S // ATTN_TILE, -1))
        x = _outproj(x, mod, o_ml, z_ml, o_mb, z_mb, g_a, g_b,
                     w_o_mla[l].astype(BF16), w_o_moba[l].astype(BF16), w_out[l].astype(BF16),
                     g_post[l][None])
    return x
```

```python
import functools

import jax
import jax.numpy as jnp
from jax import lax
from jax.experimental import pallas as pl
from jax.experimental.pallas import tpu as pltpu

EPS = 1e-6
ROPE_THETA = 10000.0
MLA_HEADS = 8
MLA_NOPE = 64
MLA_ROPE = 32
MLA_V = 64
MLA_Q_RANK = 256
MLA_KV_RANK = 128
MOBA_HEADS = 8
MOBA_DH = 64
MOBA_BLOCK = 256
MOBA_TOPK = 3

LANES = 128
HEAD_SLAB = LANES
ATTN_TILE = MOBA_BLOCK
Q_BLOCKS = 2
ATTN_HEADS = 4
MASK_FLOOR = -1e30
SUM_ROWS = 16
LOG2E = 1.4426950408889634
TOKEN_TILE = 512
ROPE_TILE = 512
VMEM_LIMIT = 56 * 1024 * 1024

F32 = jnp.float32
BF16 = jnp.bfloat16
NEG_INF = float("-inf")


def _sigmoid(t):
    return 1.0 / (1.0 + jnp.exp(-t))


def _silu(t):
    return t * _sigmoid(t)


def _adaln_kernel(c_ref, w_ref, b_ref, o_ref):
    c = c_ref[...]
    o_ref[...] = jnp.dot(_silu(c), w_ref[...], preferred_element_type=F32) + b_ref[...]


def _adaln(c, w_ada, b_ada):
    B, D = c.shape
    n = w_ada.shape[1] // D
    return pl.pallas_call(
        _adaln_kernel,
        grid=(n,),
        in_specs=[pl.BlockSpec((B, D), lambda j: (0, 0)),
                  pl.BlockSpec((D, D), lambda j: (0, j)),
                  pl.BlockSpec((1, D), lambda j: (0, j))],
        out_specs=pl.BlockSpec((B, D), lambda j: (0, j)),
        out_shape=jax.ShapeDtypeStruct((B, n * D), F32),
        compiler_params=pltpu.CompilerParams(vmem_limit_bytes=VMEM_LIMIT),
        name="adaln",
    )(c, w_ada, b_ada.reshape(1, -1))


def _rope_kernel(pos_ref, fmb_ref, fml_ref, cmb_ref, smb_ref, cml_ref, sml_ref):
    nb, ts = pos_ref.shape
    ones = jnp.ones((2 * MLA_ROPE, ts), F32)
    zeros64 = jnp.zeros((2 * MLA_ROPE, ts), F32)
    zeros32 = jnp.zeros((MLA_ROPE, ts), F32)
    for b in range(nb):
        pos = pos_ref[b:b + 1, :].astype(F32)
        amb = fmb_ref[...] * pos
        cb, sb = jnp.cos(amb), jnp.sin(amb)
        cmb_ref[b] = jnp.concatenate([cb, cb, cb, cb], axis=0).T
        smb_ref[b] = jnp.concatenate([-sb, sb, -sb, sb], axis=0).T
        aml = fml_ref[...] * pos
        ca, sa = jnp.cos(aml), jnp.sin(aml)
        cml_ref[b] = jnp.concatenate([ones, ca, ca, zeros32], axis=0).T
        sml_ref[b] = jnp.concatenate([zeros64, -sa, sa, zeros32], axis=0).T


def _rope_tables(positions):
    B, S = positions.shape
    ts = ROPE_TILE
    half_mb, half_ml = MOBA_DH // 2, MLA_ROPE // 2
    fmb = (ROPE_THETA ** (-jnp.arange(half_mb, dtype=F32) / half_mb)).reshape(half_mb, 1)
    fml = (ROPE_THETA ** (-jnp.arange(half_ml, dtype=F32) / half_ml)).reshape(half_ml, 1)
    tab = jax.ShapeDtypeStruct((B, S, LANES), F32)
    tab_spec = pl.BlockSpec((B, ts, LANES), lambda j: (0, j, 0))
    return pl.pallas_call(
        _rope_kernel,
        grid=(S // ts,),
        in_specs=[pl.BlockSpec((B, ts), lambda j: (0, j)),
                  pl.BlockSpec((half_mb, 1), lambda j: (0, 0)),
                  pl.BlockSpec((half_ml, 1), lambda j: (0, 0))],
        out_specs=[tab_spec] * 4,
        out_shape=[tab] * 4,
        compiler_params=pltpu.CompilerParams(vmem_limit_bytes=VMEM_LIMIT),
        name="rope_tables",
    )(positions, fmb, fml)


def _rope_slab(t, cos, sin, half):
    lane = lax.broadcasted_iota(jnp.int32, t.shape, 1)
    first = (lane & (2 * half - 1)) < half
    partner = jnp.where(first, pltpu.roll(t, LANES - half, 1), pltpu.roll(t, half, 1))
    return t * cos + partner * sin


def _rms(t, g):
    return t * lax.rsqrt(jnp.mean(t * t, axis=-1, keepdims=True) + EPS) * g


def _inproj_kernel(x_ref, mod_ref, gpre_ref, w_ref, gq_ref, wuq_ref, gkv_ref, wuk_ref, wuv_ref,
                   bm_ref, cmb_ref, smb_ref, cml_ref, sml_ref,
                   qml_ref, kml_ref, vtml_ref, zml_ref, qmb_ref, kmb_ref, vtmb_ref, zmb_ref,
                   ga_ref, gb_ref, kmean_ref):
    tm = x_ref.shape[1]
    nsub = tm // ATTN_TILE
    x = x_ref[0]
    shift = mod_ref[0, 0:1, :]
    scale = mod_ref[0, 1:2, :]
    h = (_rms(x, gpre_ref[...]) * (1.0 + scale) + shift).astype(BF16)

    def proj(lo, hi):
        return jnp.dot(h, w_ref[:, lo:hi], preferred_element_type=F32)

    cmb, smb = cmb_ref[0], smb_ref[0]
    cml, sml = cml_ref[0], sml_ref[0]

    g0 = proj(0, 512)
    q_scale = float((MLA_NOPE + MLA_ROPE) ** -0.5) * LOG2E
    qn = _rms(g0[:, :MLA_Q_RANK], gq_ref[...] * q_scale).astype(BF16)
    kvn = _rms(g0[:, MLA_Q_RANK:MLA_Q_RANK + MLA_KV_RANK], gkv_ref[...]).astype(BF16)
    kpe = _rope_slab(g0[:, 384:512], cml, sml, MLA_ROPE // 2)
    q = jnp.dot(qn, wuq_ref[...], preferred_element_type=F32)
    kn = jnp.dot(kvn, wuk_ref[...], preferred_element_type=F32)
    for hd in range(MLA_HEADS):
        sl = slice(hd * HEAD_SLAB, (hd + 1) * HEAD_SLAB)
        qml_ref[0, :, sl] = _rope_slab(q[:, sl], cml, sml, MLA_ROPE // 2).astype(BF16)
        kml_ref[0, :, sl] = (kn[:, sl] + kpe).astype(BF16)
    v = jnp.dot(kvn, wuv_ref[...], preferred_element_type=F32)
    for r in range(nsub):
        vtml_ref[0, r] = v[r * ATTN_TILE:(r + 1) * ATTN_TILE, :].T.astype(BF16)
    zml_ref[0] = _silu(proj(512, 1024)).astype(BF16)

    qm = proj(1024, 1536)
    km = proj(1536, 2048)
    mb_scale = float(MOBA_DH ** -0.5) * LOG2E
    for cidx in range(MOBA_HEADS * MOBA_DH // LANES):
        sl = slice(cidx * LANES, (cidx + 1) * LANES)
        qmb_ref[0, :, sl] = (_rope_slab(qm[:, sl], cmb, smb, MOBA_DH // 2) * mb_scale).astype(BF16)
        kr = _rope_slab(km[:, sl], cmb, smb, MOBA_DH // 2)
        kmb_ref[0, :, sl] = kr.astype(BF16)
        for r in range(nsub):
            kmean_ref[0, r, :, sl] = jnp.mean(kr[r * ATTN_TILE:(r + 1) * ATTN_TILE, :], axis=0, keepdims=True)
    vm = proj(2048, 2560)
    for r in range(nsub):
        vtmb_ref[0, r] = vm[r * ATTN_TILE:(r + 1) * ATTN_TILE, :].T.astype(BF16)
    zmb_ref[0] = _silu(proj(2560, 3072)).astype(BF16)

    D = x_ref.shape[2]
    ga_ref[0] = _sigmoid(proj(3072, 3072 + D) + bm_ref[:, :D]).astype(BF16)
    gb_ref[0] = _sigmoid(proj(3072 + D, 3072 + 2 * D) + bm_ref[:, D:]).astype(BF16)


def _prep_in_weights(w_in, w_uq, w_ukv):
    D = w_in.shape[0]
    o = 0
    cols = {}
    for name, n in (("q_lat", MLA_Q_RANK), ("kv_lat", MLA_KV_RANK), ("k_rope", MLA_ROPE),
                    ("z_mla", 512), ("q_mb", 512), ("k_mb", 512), ("v_mb", 512), ("z_mb", 512),
                    ("merge", 2 * D)):
        cols[name] = w_in[:, o:o + n]
        o += n
    zpad = lambda n: jnp.zeros((D, n), w_in.dtype)
    w = jnp.concatenate([cols["q_lat"], cols["kv_lat"],
                         zpad(MLA_NOPE), cols["k_rope"], zpad(HEAD_SLAB - MLA_NOPE - MLA_ROPE),
                         cols["z_mla"], cols["q_mb"], cols["k_mb"], cols["v_mb"], cols["z_mb"],
                         cols["merge"]], axis=1).astype(BF16)
    dq = MLA_NOPE + MLA_ROPE
    wuq = w_uq.reshape(MLA_Q_RANK, MLA_HEADS, dq)
    wuq = jnp.pad(wuq, ((0, 0), (0, 0), (0, HEAD_SLAB - dq))).reshape(MLA_Q_RANK, MLA_HEADS * HEAD_SLAB)
    wukv = w_ukv.reshape(MLA_KV_RANK, MLA_HEADS, MLA_NOPE + MLA_V)
    wuk = jnp.pad(wukv[:, :, :MLA_NOPE], ((0, 0), (0, 0), (0, HEAD_SLAB - MLA_NOPE)))
    wuk = wuk.reshape(MLA_KV_RANK, MLA_HEADS * HEAD_SLAB)
    wuv = wukv[:, :, MLA_NOPE:].reshape(MLA_KV_RANK, MLA_HEADS * MLA_V)
    return w, wuq.astype(BF16), wuk.astype(BF16), wuv.astype(BF16)


def _inproj(x, mod, g_pre, w, g_q, wuq, g_kv, wuk, wuv, b_merge, tabs):
    B, S, D = x.shape
    tm = TOKEN_TILE
    nkb = S // ATTN_TILE
    nsub = tm // ATTN_TILE
    wide = MLA_HEADS * HEAD_SLAB
    mbw = MOBA_HEADS * MOBA_DH

    def const(shape):
        return pl.BlockSpec(shape, lambda b, i: (0,) * len(shape), pipeline_mode=pl.Buffered(1))

    def rows(n):
        return pl.BlockSpec((1, tm, n), lambda b, i: (b, i, 0))

    vt_spec = pl.BlockSpec((1, nsub, mbw, ATTN_TILE), lambda b, i: (b, i, 0, 0))
    bf = lambda *s: jax.ShapeDtypeStruct(s, BF16)
    out_shape = [bf(B, S, wide), bf(B, S, wide), bf(B, nkb, mbw, ATTN_TILE), bf(B, S, mbw),
                 bf(B, S, mbw), bf(B, S, mbw), bf(B, nkb, mbw, ATTN_TILE), bf(B, S, mbw),
                 bf(B, S, D), bf(B, S, D),
                 jax.ShapeDtypeStruct((B, nkb, 1, mbw), F32)]
    out_specs = [rows(wide), rows(wide), vt_spec, rows(mbw),
                 rows(mbw), rows(mbw), vt_spec, rows(mbw),
                 rows(D), rows(D),
                 pl.BlockSpec((1, nsub, 1, mbw), lambda b, i: (b, i, 0, 0))]
    in_specs = [rows(D),
                pl.BlockSpec((1, 3, D), lambda b, i: (b, 0, 0)),
                const((1, D)), const(w.shape), const((1, MLA_Q_RANK)), const(wuq.shape),
                const((1, MLA_KV_RANK)), const(wuk.shape), const(wuv.shape), const((1, 2 * D)),
                rows(LANES), rows(LANES), rows(LANES), rows(LANES)]
    return pl.pallas_call(
        _inproj_kernel,
        grid=(B, S // tm),
        in_specs=in_specs,
        out_specs=out_specs,
        out_shape=out_shape,
        compiler_params=pltpu.CompilerParams(
            dimension_semantics=("parallel", "parallel"), vmem_limit_bytes=VMEM_LIMIT),
        name="inproj",
    )(x, mod, g_pre, w, g_q, wuq, g_kv, wuk, wuv, b_merge, *tabs)


def _attn_kernel(*refs, moba):
    if moba:
        (q_ref, k_ref, vt_ref, kmean_ref, o_ref,
         s0, s1, p0, p1, cm0, cm1, al0, al1, m_ref, acc_ref, bias_ref, qm_ref) = refs
    else:
        (q_ref, k_ref, vt_ref, o_ref, s0, s1, p0, p1, cm0, cm1, al0, al1, m_ref, acc_ref) = refs
    t = ATTN_TILE
    tq = Q_BLOCKS * t
    nh = ATTN_HEADS
    a = pl.program_id(2)
    own0 = Q_BLOCKS * a
    dv = vt_ref.shape[2] // nh
    nkb = vt_ref.shape[1]
    qcol = lax.broadcasted_iota(jnp.int32, (1, tq), 1)
    row = lax.broadcasted_iota(jnp.int32, (t, tq), 0)
    col = lax.broadcasted_iota(jnp.int32, (t, tq), 1)

    def q_head(h, first_row=0):
        if moba:
            return qm_ref[h, first_row:, :]
        return q_ref[0, first_row:, h * HEAD_SLAB:(h + 1) * HEAD_SLAB]

    def k_block(kb, h):
        start = pl.multiple_of(kb * t, t)
        lanes = (h // 2) * LANES if moba else h * HEAD_SLAB
        return k_ref[0, pl.ds(start, t), lanes:lanes + LANES]

    if moba:
        lane = lax.broadcasted_iota(jnp.int32, (tq, LANES), 1)
        for h in range(nh):
            q2 = q_ref[0, :, (h // 2) * LANES:(h // 2 + 1) * LANES]
            keep = (lane < MOBA_DH) if h % 2 == 0 else (lane >= MOBA_DH)
            qm_ref[h] = jnp.where(keep, q2, jnp.zeros_like(q2))
        blk = lax.broadcasted_iota(jnp.int32, (nkb, tq), 0).astype(F32)
        own = (own0 + jnp.where(qcol >= t, 1, 0)).astype(F32)
        for h in range(nh):
            km = kmean_ref[0, :, (h // 2) * LANES:(h // 2 + 1) * LANES]
            g = lax.dot_general(km, qm_ref[h].astype(F32), (((1,), (1,)), ((), ())),
                                preferred_element_type=F32)
            g = jnp.where(blk < own, g, NEG_INF)
            bias = jnp.full((nkb, tq), NEG_INF, F32)
            for _ in range(MOBA_TOPK):
                best = jnp.max(g, axis=0, keepdims=True)
                hit = (g == best) & (best > NEG_INF)
                first = jnp.min(jnp.where(hit, blk, float(nkb)), axis=0, keepdims=True)
                pick = blk == first
                bias = jnp.where(pick, 0.0, bias)
                g = jnp.where(pick, NEG_INF, g)
            bias_ref[h] = jnp.where(blk == own, 0.0, bias)

    def stage_a(kb, s_ref, cm_ref):
        for h in range(nh):
            s = lax.dot_general(k_block(kb, h), q_head(h), (((1,), (1,)), ((), ())),
                                preferred_element_type=F32)
            s_ref[h] = s
            cm = jnp.max(s, axis=0, keepdims=True)
            if moba:
                cm = cm + bias_ref[h, pl.ds(kb, 1), :]
            cm_ref[h] = cm

    def stage_b(s_ref, cm_ref, p_ref, al_ref, kb=None):
        for h in range(nh):
            m_old = m_ref[h]
            m_new = jnp.maximum(m_old, cm_ref[h])
            m_ref[h] = m_new
            al_ref[h] = jnp.exp2(m_old - m_new)
            if moba and kb is not None:
                m_new = m_new - bias_ref[h, pl.ds(kb, 1), :]
            p_ref[h] = jnp.exp2(s_ref[h] - m_new).astype(BF16)

    ones_rows = jnp.ones((SUM_ROWS, t), BF16)

    def stage_c(kb, p_ref, al_ref):
        for h in range(nh):
            vt = jnp.concatenate([vt_ref[0, kb, h * dv:(h + 1) * dv, :], ones_rows], axis=0)
            acc_ref[h] = al_ref[h] * acc_ref[h] + jnp.dot(vt, p_ref[h], preferred_element_type=F32)

    m_ref[...] = jnp.full(m_ref.shape, MASK_FLOOR, F32)
    acc_ref[...] = jnp.zeros(acc_ref.shape, F32)

    tri = jnp.where(row <= col, 0.0, NEG_INF).astype(F32)[:, :t]
    for h in range(nh):
        s = lax.dot_general(k_block(own0, h), q_head(h), (((1,), (1,)), ((), ())),
                            preferred_element_type=F32)
        if moba:
            s = s + bias_ref[h, pl.ds(own0, 1), :]
        s = jnp.concatenate([s[:, :t] + tri, s[:, t:]], axis=1)
        s0[h] = s
        cm0[h] = jnp.max(s, axis=0, keepdims=True)
    for h in range(nh):
        s = lax.dot_general(k_block(own0 + 1, h), q_head(h, t), (((1,), (1,)), ((), ())),
                            preferred_element_type=F32) + tri
        s1[h, :, t:] = s
        cm1[h, :, t:] = jnp.max(s, axis=0, keepdims=True)
    stage_b(s0, cm0, p0, al0)
    for h in range(nh):
        m_old = m_ref[h, :, t:]
        m_new = jnp.maximum(m_old, cm1[h, :, t:])
        m_ref[h, :, t:] = m_new
        al1[h] = jnp.concatenate([jnp.ones((1, t), F32), jnp.exp2(m_old - m_new)], axis=1)
        p1[h, :, :t] = jnp.zeros((t, t), BF16)
        p1[h, :, t:] = jnp.exp2(s1[h, :, t:] - m_new).astype(BF16)
    stage_c(own0, p0, al0)
    stage_a(0, s0, cm0)

    def pair(up, carry):
        kb = Q_BLOCKS * up
        stage_a(kb + 1, s1, cm1)
        stage_b(s0, cm0, p0, al0, kb)
        stage_c(jnp.where(up == 0, own0 + 1, kb - 1), p1, al1)
        stage_a(kb + 2, s0, cm0)
        stage_b(s1, cm1, p1, al1, kb + 1)
        stage_c(kb, p0, al0)
        return carry

    lax.fori_loop(0, a, pair, 0)
    stage_c(jnp.where(a == 0, own0 + 1, own0 - 1), p1, al1)

    outs = []
    for h in range(nh):
        acc = acc_ref[h]
        outs.append(acc[:dv] * (1.0 / acc[dv:dv + 1]))
    o_ref[0] = jnp.concatenate(outs, axis=0).T.astype(o_ref.dtype)


def _attention(q, k, vt, kmean=None):
    B, S, _ = q.shape
    t = ATTN_TILE
    tq = Q_BLOCKS * t
    nh = ATTN_HEADS
    nkb = S // t
    groups = vt.shape[2] // (nh * MLA_V)
    wq = q.shape[2] // groups
    dv = vt.shape[2] // (groups * nh)
    moba = kmean is not None
    assert S % tq == 0 and dv == MLA_V == MOBA_DH
    in_specs = [pl.BlockSpec((1, tq, wq), lambda b, g, i: (b, i, g)),
                pl.BlockSpec((1, S, wq), lambda b, g, i: (b, 0, g)),
                pl.BlockSpec((1, nkb, nh * dv, t), lambda b, g, i: (b, 0, g, 0))]
    args = [q, k, vt]
    row_vec = pltpu.VMEM((nh, 1, tq), F32)
    scratch = [pltpu.VMEM((nh, t, tq), F32), pltpu.VMEM((nh, t, tq), F32),
               pltpu.VMEM((nh, t, tq), BF16), pltpu.VMEM((nh, t, tq), BF16),
               row_vec, row_vec, row_vec, row_vec,
               row_vec,
               pltpu.VMEM((nh, dv + SUM_ROWS, tq), F32)]
    if moba:
        in_specs.append(pl.BlockSpec((1, nkb, wq), lambda b, g, i: (b, 0, g)))
        args.append(kmean)
        scratch += [pltpu.VMEM((nh, nkb, tq), F32),
                    pltpu.VMEM((nh, tq, LANES), BF16)]
    return pl.pallas_call(
        functools.partial(_attn_kernel, moba=moba),
        grid=(B, groups, S // tq),
        in_specs=in_specs,
        out_specs=pl.BlockSpec((1, tq, nh * dv), lambda b, g, i: (b, i, g)),
        out_shape=jax.ShapeDtypeStruct((B, S, groups * nh * dv), BF16),
        scratch_shapes=scratch,
        compiler_params=pltpu.CompilerParams(
            dimension_semantics=("parallel", "parallel", "parallel"), vmem_limit_bytes=VMEM_LIMIT),
        name="moba_attn" if moba else "mla_attn",
    )(*args)


def _outproj_kernel(x_ref, mod_ref, oml_ref, zml_ref, omb_ref, zmb_ref, ga_ref, gb_ref,
                    woa_ref, wob_ref, wout_ref, gpost_ref, o_ref):
    y_a = jnp.dot(oml_ref[0] * zml_ref[0], woa_ref[...], preferred_element_type=F32)
    y_b = jnp.dot(omb_ref[0] * zmb_ref[0], wob_ref[...], preferred_element_type=F32)
    u = ga_ref[0].astype(F32) * y_a + gb_ref[0].astype(F32) * y_b
    y = jnp.dot(u.astype(BF16), wout_ref[...], preferred_element_type=F32)
    gate = mod_ref[0, 2:3, :]
    o_ref[0] = x_ref[0] + gate * _rms(y, gpost_ref[...])


def _outproj(x, mod, o_mla, z_mla, o_mb, z_mb, g_a, g_b, woa, wob, wout, g_post):
    B, S, D = x.shape
    tm = TOKEN_TILE
    width = o_mla.shape[2]

    def const(shape):
        return pl.BlockSpec(shape, lambda b, i: (0,) * len(shape), pipeline_mode=pl.Buffered(1))

    def rows(n):
        return pl.BlockSpec((1, tm, n), lambda b, i: (b, i, 0))

    return pl.pallas_call(
        _outproj_kernel,
        grid=(B, S // tm),
        in_specs=[rows(D), pl.BlockSpec((1, 3, D), lambda b, i: (b, 0, 0)),
                  rows(width), rows(width), rows(width), rows(width), rows(D), rows(D),
                  const(woa.shape), const(wob.shape), const(wout.shape), const((1, D))],
        out_specs=rows(D),
        out_shape=jax.ShapeDtypeStruct((B, S, D), x.dtype),
        compiler_params=pltpu.CompilerParams(
            dimension_semantics=("parallel", "parallel"), vmem_limit_bytes=VMEM_LIMIT),
        name="outproj",
    )(x, mod, o_mla, z_mla, o_mb, z_mb, g_a, g_b, woa, wob, wout, g_post)


def kernel(x, c, positions, w_ada, b_ada, g_pre, g_post, w_in, g_q_lat, w_uq, g_kv_lat, w_ukv,
           w_o_mla, w_o_moba, b_merge, w_out):
    B, S, D = x.shape
    depth = w_ada.shape[0]
    assert S % TOKEN_TILE == 0 and S % ROPE_TILE == 0 and TOKEN_TILE % ATTN_TILE == 0
    assert w_in.shape[2] == (MLA_Q_RANK + MLA_KV_RANK + MLA_ROPE + 5 * 512 + 2 * D)
    tabs = _rope_tables(positions)
    for l in range(depth):
        mod = _adaln(c, w_ada[l], b_ada[l]).reshape(B, 3, D)
        w, wuq, wuk, wuv = _prep_in_weights(w_in[l], w_uq[l], w_ukv[l])
        (q_ml, k_ml, vt_ml, z_ml, q_mb, k_mb, vt_mb, z_mb, g_a, g_b, kmean) = _inproj(
            x, mod, g_pre[l][None], w, g_q_lat[l][None], wuq, g_kv_lat[l][None], wuk, wuv,
            b_merge[l][None], tabs)
        o_ml = _attention(q_ml, k_ml, vt_ml)
        o_mb = _attention(q_mb, k_mb, vt_mb, kmean.reshape(B, S // ATTN_TILE, -1))
        x = _outproj(x, mod, o_ml, z_ml, o_mb, z_mb, g_a, g_b,
                     w_o_mla[l].astype(BF16), w_o_moba[l].astype(BF16), w_out[l].astype(BF16),
                     g_post[l][None])
    return x
```

```python
import functools

import jax
import jax.numpy as jnp
from jax import lax
from jax.experimental import pallas as pl
from jax.experimental.pallas import tpu as pltpu

EPS = 1e-6
ROPE_THETA = 10000.0
MLA_HEADS = 8
MLA_NOPE = 64
MLA_ROPE = 32
MLA_V = 64
MLA_Q_RANK = 256
MLA_KV_RANK = 128
MOBA_HEADS = 8
MOBA_DH = 64
MOBA_BLOCK = 256
MOBA_TOPK = 3

LANES = 128
HEAD_SLAB = LANES
ATTN_TILE = MOBA_BLOCK
Q_BLOCKS = 2
ATTN_HEADS = 4
MASK_FLOOR = -1e30
SUM_ROWS = 16
LOG2E = 1.4426950408889634
TOKEN_TILE = 512
ROPE_TILE = 512
VMEM_LIMIT = 56 * 1024 * 1024

F32 = jnp.float32
BF16 = jnp.bfloat16
NEG_INF = float("-inf")


def _sigmoid(t):
    return 1.0 / (1.0 + jnp.exp(-t))


def _silu(t):
    return t * _sigmoid(t)


def _adaln_kernel(c_ref, w_ref, b_ref, o_ref):
    c = c_ref[...]
    o_ref[...] = jnp.dot(_silu(c), w_ref[...], preferred_element_type=F32) + b_ref[...]


def _adaln(c, w_ada, b_ada):
    B, D = c.shape
    n = w_ada.shape[1] // D
    return pl.pallas_call(
        _adaln_kernel,
        grid=(n,),
        in_specs=[pl.BlockSpec((B, D), lambda j: (0, 0)),
                  pl.BlockSpec((D, D), lambda j: (0, j)),
                  pl.BlockSpec((1, D), lambda j: (0, j))],
        out_specs=pl.BlockSpec((B, D), lambda j: (0, j)),
        out_shape=jax.ShapeDtypeStruct((B, n * D), F32),
        compiler_params=pltpu.CompilerParams(vmem_limit_bytes=VMEM_LIMIT),
        name="adaln",
    )(c, w_ada, b_ada.reshape(1, -1))


def _rope_kernel(pos_ref, fmb_ref, fml_ref, cmb_ref, smb_ref, cml_ref, sml_ref):
    nb, ts = pos_ref.shape
    ones = jnp.ones((2 * MLA_ROPE, ts), F32)
    zeros64 = jnp.zeros((2 * MLA_ROPE, ts), F32)
    zeros32 = jnp.zeros((MLA_ROPE, ts), F32)
    for b in range(nb):
        pos = pos_ref[b:b + 1, :].astype(F32)
        amb = fmb_ref[...] * pos
        cb, sb = jnp.cos(amb), jnp.sin(amb)
        cmb_ref[b] = jnp.concatenate([cb, cb, cb, cb], axis=0).T
        smb_ref[b] = jnp.concatenate([-sb, sb, -sb, sb], axis=0).T
        aml = fml_ref[...] * pos
        ca, sa = jnp.cos(aml), jnp.sin(aml)
        cml_ref[b] = jnp.concatenate([ones, ca, ca, zeros32], axis=0).T
        sml_ref[b] = jnp.concatenate([zeros64, -sa, sa, zeros32], axis=0).T


def _rope_tables(positions):
    B, S = positions.shape
    ts = ROPE_TILE
    half_mb, half_ml = MOBA_DH // 2, MLA_ROPE // 2
    fmb = (ROPE_THETA ** (-jnp.arange(half_mb, dtype=F32) / half_mb)).reshape(half_mb, 1)
    fml = (ROPE_THETA ** (-jnp.arange(half_ml, dtype=F32) / half_ml)).reshape(half_ml, 1)
    tab = jax.ShapeDtypeStruct((B, S, LANES), F32)
    tab_spec = pl.BlockSpec((B, ts, LANES), lambda j: (0, j, 0))
    return pl.pallas_call(
        _rope_kernel,
        grid=(S // ts,),
        in_specs=[pl.BlockSpec((B, ts), lambda j: (0, j)),
                  pl.BlockSpec((half_mb, 1), lambda j: (0, 0)),
                  pl.BlockSpec((half_ml, 1), lambda j: (0, 0))],
        out_specs=[tab_spec] * 4,
        out_shape=[tab] * 4,
        compiler_params=pltpu.CompilerParams(vmem_limit_bytes=VMEM_LIMIT),
        name="rope_tables",
    )(positions, fmb, fml)


def _rope_slab(t, cos, sin, half):
    lane = lax.broadcasted_iota(jnp.int32, t.shape, 1)
    first = (lane & (2 * half - 1)) < half
    partner = jnp.where(first, pltpu.roll(t, LANES - half, 1), pltpu.roll(t, half, 1))
    return t * cos + partner * sin


def _rms(t, g):
    return t * lax.rsqrt(jnp.mean(t * t, axis=-1, keepdims=True) + EPS) * g


def _inproj_kernel(x_ref, mod_ref, gpre_ref, w_ref, gq_ref, wuq_ref, gkv_ref, wuk_ref, wuv_ref,
                   bm_ref, cmb_ref, smb_ref, cml_ref, sml_ref,
                   qml_ref, kml_ref, vtml_ref, zml_ref, qmb_ref, kmb_ref, vtmb_ref, zmb_ref,
                   ga_ref, gb_ref, kmean_ref):
    tm = x_ref.shape[1]
    nsub = tm // ATTN_TILE
    x = x_ref[0]
    shift = mod_ref[0, 0:1, :]
    scale = mod_ref[0, 1:2, :]
    h = (_rms(x, gpre_ref[...]) * (1.0 + scale) + shift).astype(BF16)

    def proj(lo, hi):
        return jnp.dot(h, w_ref[:, lo:hi], preferred_element_type=F32)

    cmb, smb = cmb_ref[0], smb_ref[0]
    cml, sml = cml_ref[0], sml_ref[0]

    g0 = proj(0, 512)
    q_scale = float((MLA_NOPE + MLA_ROPE) ** -0.5) * LOG2E
    qn = _rms(g0[:, :MLA_Q_RANK], gq_ref[...] * q_scale).astype(BF16)
    kvn = _rms(g0[:, MLA_Q_RANK:MLA_Q_RANK + MLA_KV_RANK], gkv_ref[...]).astype(BF16)
    kpe = _rope_slab(g0[:, 384:512], cml, sml, MLA_ROPE // 2)
    q = jnp.dot(qn, wuq_ref[...], preferred_element_type=F32)
    kn = jnp.dot(kvn, wuk_ref[...], preferred_element_type=F32)
    for hd in range(MLA_HEADS):
        sl = slice(hd * HEAD_SLAB, (hd + 1) * HEAD_SLAB)
        qml_ref[0, :, sl] = _rope_slab(q[:, sl], cml, sml, MLA_ROPE // 2).astype(BF16)
        kml_ref[0, :, sl] = (kn[:, sl] + kpe).astype(BF16)
    v = jnp.dot(kvn, wuv_ref[...], preferred_element_type=F32)
    for r in range(nsub):
        vtml_ref[0, r] = v[r * ATTN_TILE:(r + 1) * ATTN_TILE, :].T.astype(BF16)
    zml_ref[0] = _silu(proj(512, 1024)).astype(BF16)

    qm = proj(1024, 1536)
    km = proj(1536, 2048)
    mb_scale = float(MOBA_DH ** -0.5) * LOG2E
    for cidx in range(MOBA_HEADS * MOBA_DH // LANES):
        sl = slice(cidx * LANES, (cidx + 1) * LANES)
        qmb_ref[0, :, sl] = (_rope_slab(qm[:, sl], cmb, smb, MOBA_DH // 2) * mb_scale).astype(BF16)
        kr = _rope_slab(km[:, sl], cmb, smb, MOBA_DH // 2)
        kmb_ref[0, :, sl] = kr.astype(BF16)
        for r in range(nsub):
            kmean_ref[0, r, :, sl] = jnp.mean(kr[r * ATTN_TILE:(r + 1) * ATTN_TILE, :], axis=0, keepdims=True)
    vm = proj(2048, 2560)
    for r in range(nsub):
        vtmb_ref[0, r] = vm[r * ATTN_TILE:(r + 1) * ATTN_TILE, :].T.astype(BF16)
    zmb_ref[0] = _silu(proj(2560, 3072)).astype(BF16)

    D = x_ref.shape[2]
    ga_ref[0] = _sigmoid(proj(3072, 3072 + D) + bm_ref[:, :D]).astype(BF16)
    gb_ref[0] = _sigmoid(proj(3072 + D, 3072 + 2 * D) + bm_ref[:, D:]).astype(BF16)


def _prep_in_weights(w_in, w_uq, w_ukv):
    D = w_in.shape[0]
    o = 0
    cols = {}
    for name, n in (("q_lat", MLA_Q_RANK), ("kv_lat", MLA_KV_RANK), ("k_rope", MLA_ROPE),
                    ("z_mla", 512), ("q_mb", 512), ("k_mb", 512), ("v_mb", 512), ("z_mb", 512),
                    ("merge", 2 * D)):
        cols[name] = w_in[:, o:o + n]
        o += n
    zpad = lambda n: jnp.zeros((D, n), w_in.dtype)
    w = jnp.concatenate([cols["q_lat"], cols["kv_lat"],
                         zpad(MLA_NOPE), cols["k_rope"], zpad(HEAD_SLAB - MLA_NOPE - MLA_ROPE),
                         cols["z_mla"], cols["q_mb"], cols["k_mb"], cols["v_mb"], cols["z_mb"],
                         cols["merge"]], axis=1).astype(BF16)
    dq = MLA_NOPE + MLA_ROPE
    wuq = w_uq.reshape(MLA_Q_RANK, MLA_HEADS, dq)
    wuq = jnp.pad(wuq, ((0, 0), (0, 0), (0, HEAD_SLAB - dq))).reshape(MLA_Q_RANK, MLA_HEADS * HEAD_SLAB)
    wukv = w_ukv.reshape(MLA_KV_RANK, MLA_HEADS, MLA_NOPE + MLA_V)
    wuk = jnp.pad(wukv[:, :, :MLA_NOPE], ((0, 0), (0, 0), (0, HEAD_SLAB - MLA_NOPE)))
    wuk = wuk.reshape(MLA_KV_RANK, MLA_HEADS * HEAD_SLAB)
    wuv = wukv[:, :, MLA_NOPE:].reshape(MLA_KV_RANK, MLA_HEADS * MLA_V)
    return w, wuq.astype(BF16), wuk.astype(BF16), wuv.astype(BF16)


def _inproj(x, mod, g_pre, w, g_q, wuq, g_kv, wuk, wuv, b_merge, tabs):
    B, S, D = x.shape
    tm = TOKEN_TILE
    nkb = S // ATTN_TILE
    nsub = tm // ATTN_TILE
    wide = MLA_HEADS * HEAD_SLAB
    mbw = MOBA_HEADS * MOBA_DH

    def const(shape):
        return pl.BlockSpec(shape, lambda b, i: (0,) * len(shape), pipeline_mode=pl.Buffered(1))

    def rows(n):
        return pl.BlockSpec((1, tm, n), lambda b, i: (b, i, 0))

    vt_spec = pl.BlockSpec((1, nsub, mbw, ATTN_TILE), lambda b, i: (b, i, 0, 0))
    bf = lambda *s: jax.ShapeDtypeStruct(s, BF16)
    out_shape = [bf(B, S, wide), bf(B, S, wide), bf(B, nkb, mbw, ATTN_TILE), bf(B, S, mbw),
                 bf(B, S, mbw), bf(B, S, mbw), bf(B, nkb, mbw, ATTN_TILE), bf(B, S, mbw),
                 bf(B, S, D), bf(B, S, D),
                 jax.ShapeDtypeStruct((B, nkb, 1, mbw), F32)]
    out_specs = [rows(wide), rows(wide), vt_spec, rows(mbw),
                 rows(mbw), rows(mbw), vt_spec, rows(mbw),
                 rows(D), rows(D),
                 pl.BlockSpec((1, nsub, 1, mbw), lambda b, i: (b, i, 0, 0))]
    in_specs = [rows(D),
                pl.BlockSpec((1, 3, D), lambda b, i: (b, 0, 0)),
                const((1, D)), const(w.shape), const((1, MLA_Q_RANK)), const(wuq.shape),
                const((1, MLA_KV_RANK)), const(wuk.shape), const(wuv.shape), const((1, 2 * D)),
                rows(LANES), rows(LANES), rows(LANES), rows(LANES)]
    return pl.pallas_call(
        _inproj_kernel,
        grid=(B, S // tm),
        in_specs=in_specs,
        out_specs=out_specs,
        out_shape=out_shape,
        compiler_params=pltpu.CompilerParams(
            dimension_semantics=("parallel", "parallel"), vmem_limit_bytes=VMEM_LIMIT),
        name="inproj",
    )(x, mod, g_pre, w, g_q, wuq, g_kv, wuk, wuv, b_merge, *tabs)


def _attn_kernel(*refs, moba):
    if moba:
        (q_ref, k_ref, vt_ref, z_ref, kmean_ref, o_ref,
         s0, s1, s2, p0, p1, cm0, cm1, cm2, al0, al1, m_ref, acc_ref, bias_ref, qm_ref) = refs
    else:
        (q_ref, k_ref, vt_ref, z_ref, o_ref,
         s0, s1, s2, p0, p1, cm0, cm1, cm2, al0, al1, m_ref, acc_ref) = refs
    t = ATTN_TILE
    tq = Q_BLOCKS * t
    nh = s0.shape[0]
    a = pl.program_id(2)
    own0 = Q_BLOCKS * a
    dv = vt_ref.shape[2] // nh
    nkb = vt_ref.shape[1]
    qcol = lax.broadcasted_iota(jnp.int32, (1, tq), 1)
    row = lax.broadcasted_iota(jnp.int32, (t, tq), 0)
    col = lax.broadcasted_iota(jnp.int32, (t, tq), 1)

    def q_head(h, first_row=0):
        if moba:
            return qm_ref[h, first_row:, :]
        return q_ref[0, first_row:, h * HEAD_SLAB:(h + 1) * HEAD_SLAB]

    def k_block(kb, h):
        start = pl.multiple_of(kb * t, t)
        lanes = (h // 2) * LANES if moba else h * HEAD_SLAB
        return k_ref[0, pl.ds(start, t), lanes:lanes + LANES]

    if moba:
        lane = lax.broadcasted_iota(jnp.int32, (tq, LANES), 1)
        for h in range(nh):
            q2 = q_ref[0, :, (h // 2) * LANES:(h // 2 + 1) * LANES]
            keep = (lane < MOBA_DH) if h % 2 == 0 else (lane >= MOBA_DH)
            qm_ref[h] = jnp.where(keep, q2, jnp.zeros_like(q2))
        blk = lax.broadcasted_iota(jnp.int32, (nkb, tq), 0).astype(F32)
        own = (own0 + jnp.where(qcol >= t, 1, 0)).astype(F32)
        for h in range(nh):
            km = kmean_ref[0, :, (h // 2) * LANES:(h // 2 + 1) * LANES]
            g = lax.dot_general(km, qm_ref[h].astype(F32), (((1,), (1,)), ((), ())),
                                preferred_element_type=F32)
            g = jnp.where(blk < own, g, NEG_INF)
            bias = jnp.full((nkb, tq), NEG_INF, F32)
            for _ in range(MOBA_TOPK):
                best = jnp.max(g, axis=0, keepdims=True)
                hit = (g == best) & (best > NEG_INF)
                first = jnp.min(jnp.where(hit, blk, float(nkb)), axis=0, keepdims=True)
                pick = blk == first
                bias = jnp.where(pick, 0.0, bias)
                g = jnp.where(pick, NEG_INF, g)
            bias_ref[h] = jnp.where(blk == own, 0.0, bias)

    def stage_a(kb, s_ref, cm_ref):
        for h in range(nh):
            s = lax.dot_general(k_block(kb, h), q_head(h), (((1,), (1,)), ((), ())),
                                preferred_element_type=F32)
            s_ref[h] = s
            cm = jnp.max(s, axis=0, keepdims=True)
            if moba:
                cm = cm + bias_ref[h, pl.ds(kb, 1), :]
            cm_ref[h] = cm

    def stage_b(s_ref, cm_ref, p_ref, al_ref, kb=None):
        for h in range(nh):
            m_old = m_ref[h]
            m_new = jnp.maximum(m_old, cm_ref[h])
            m_ref[h] = m_new
            al_ref[h] = jnp.exp2(m_old - m_new)
            if moba and kb is not None:
                m_new = m_new - bias_ref[h, pl.ds(kb, 1), :]
            p_ref[h] = jnp.exp2(s_ref[h] - m_new).astype(BF16)

    ones_rows = jnp.ones((SUM_ROWS, t), BF16)

    def stage_c(kb, p_ref, al_ref):
        for h in range(nh):
            vt = jnp.concatenate([vt_ref[0, kb, h * dv:(h + 1) * dv, :], ones_rows], axis=0)
            acc_ref[h] = al_ref[h] * acc_ref[h] + jnp.dot(vt, p_ref[h], preferred_element_type=F32)

    m_ref[...] = jnp.full(m_ref.shape, MASK_FLOOR, F32)
    acc_ref[...] = jnp.zeros(acc_ref.shape, F32)

    stage_a(0, s0, cm0)
    tri = jnp.where(row <= col, 0.0, NEG_INF).astype(F32)[:, :t]
    for h in range(nh):
        s = lax.dot_general(k_block(own0, h), q_head(h), (((1,), (1,)), ((), ())),
                            preferred_element_type=F32)
        if moba:
            s = s + bias_ref[h, pl.ds(own0, 1), :]
        s = jnp.concatenate([s[:, :t] + tri, s[:, t:]], axis=1)
        s2[h] = s
        cm2[h] = jnp.max(s, axis=0, keepdims=True)
    for h in range(nh):
        s = lax.dot_general(k_block(own0 + 1, h), q_head(h, t), (((1,), (1,)), ((), ())),
                            preferred_element_type=F32) + tri
        s1[h, :, t:] = s
        cm1[h, :, t:] = jnp.max(s, axis=0, keepdims=True)
    stage_b(s2, cm2, p0, al0)
    for h in range(nh):
        m_old = m_ref[h, :, t:]
        m_new = jnp.maximum(m_old, cm1[h, :, t:])
        m_ref[h, :, t:] = m_new
        al1[h] = jnp.concatenate([jnp.ones((1, t), F32), jnp.exp2(m_old - m_new)], axis=1)
        p1[h, :, :t] = jnp.zeros((t, t), BF16)
        p1[h, :, t:] = jnp.exp2(s1[h, :, t:] - m_new).astype(BF16)
    stage_c(own0, p0, al0)

    def pair(up, carry):
        kb = Q_BLOCKS * up
        stage_a(kb + 1, s1, cm1)
        stage_b(s0, cm0, p0, al0, kb)
        stage_c(jnp.where(up == 0, own0 + 1, kb - 1), p1, al1)
        stage_a(kb + 2, s0, cm0)
        stage_b(s1, cm1, p1, al1, kb + 1)
        stage_c(kb, p0, al0)
        return carry

    lax.fori_loop(0, a, pair, 0)
    stage_c(jnp.where(a == 0, own0 + 1, own0 - 1), p1, al1)

    outs = []
    for h in range(nh):
        acc = acc_ref[h]
        outs.append(acc[:dv] * (1.0 / acc[dv:dv + 1]))
    o_ref[0] = (jnp.concatenate(outs, axis=0).T * z_ref[0].astype(F32)).astype(o_ref.dtype)


def _attention(q, k, vt, z, kmean=None):
    B, S, _ = q.shape
    t = ATTN_TILE
    tq = Q_BLOCKS * t
    nh = ATTN_HEADS
    nkb = S // t
    groups = vt.shape[2] // (nh * MLA_V)
    wq = q.shape[2] // groups
    dv = vt.shape[2] // (groups * nh)
    moba = kmean is not None
    assert S % tq == 0 and dv == MLA_V == MOBA_DH
    in_specs = [pl.BlockSpec((1, tq, wq), lambda b, g, i: (b, i, g)),
                pl.BlockSpec((1, S, wq), lambda b, g, i: (b, 0, g)),
                pl.BlockSpec((1, nkb, nh * dv, t), lambda b, g, i: (b, 0, g, 0)),
                pl.BlockSpec((1, tq, nh * dv), lambda b, g, i: (b, i, g))]
    args = [q, k, vt, z]
    row_vec = pltpu.VMEM((nh, 1, tq), F32)
    score_slot = pltpu.VMEM((nh, t, tq), F32)
    scratch = [score_slot, score_slot, score_slot,
               pltpu.VMEM((nh, t, tq), BF16), pltpu.VMEM((nh, t, tq), BF16),
               row_vec, row_vec, row_vec, row_vec, row_vec,
               row_vec,
               pltpu.VMEM((nh, dv + SUM_ROWS, tq), F32)]
    if moba:
        in_specs.append(pl.BlockSpec((1, nkb, wq), lambda b, g, i: (b, 0, g)))
        args.append(kmean)
        scratch += [pltpu.VMEM((nh, nkb, tq), F32),
                    pltpu.VMEM((nh, tq, LANES), BF16)]
    return pl.pallas_call(
        functools.partial(_attn_kernel, moba=moba),
        grid=(B, groups, S // tq),
        in_specs=in_specs,
        out_specs=pl.BlockSpec((1, tq, nh * dv), lambda b, g, i: (b, i, g)),
        out_shape=jax.ShapeDtypeStruct((B, S, groups * nh * dv), BF16),
        scratch_shapes=scratch,
        compiler_params=pltpu.CompilerParams(
            dimension_semantics=("parallel", "parallel", "parallel"), vmem_limit_bytes=VMEM_LIMIT),
        name="moba_attn" if moba else "mla_attn",
    )(*args)


def _outproj_kernel(x_ref, mod_ref, oml_ref, omb_ref, ga_ref, gb_ref,
                    woa_ref, wob_ref, wout_ref, gpost_ref, o_ref):
    y_a = jnp.dot(oml_ref[0], woa_ref[...], preferred_element_type=F32)
    y_b = jnp.dot(omb_ref[0], wob_ref[...], preferred_element_type=F32)
    u = ga_ref[0].astype(F32) * y_a + gb_ref[0].astype(F32) * y_b
    y = jnp.dot(u.astype(BF16), wout_ref[...], preferred_element_type=F32)
    gate = mod_ref[0, 2:3, :]
    o_ref[0] = x_ref[0] + gate * _rms(y, gpost_ref[...])


def _outproj(x, mod, o_mla, o_mb, g_a, g_b, woa, wob, wout, g_post):
    B, S, D = x.shape
    tm = TOKEN_TILE
    width = o_mla.shape[2]

    def const(shape):
        return pl.BlockSpec(shape, lambda b, i: (0,) * len(shape), pipeline_mode=pl.Buffered(1))

    def rows(n):
        return pl.BlockSpec((1, tm, n), lambda b, i: (b, i, 0))

    return pl.pallas_call(
        _outproj_kernel,
        grid=(B, S // tm),
        in_specs=[rows(D), pl.BlockSpec((1, 3, D), lambda b, i: (b, 0, 0)),
                  rows(width), rows(width), rows(D), rows(D),
                  const(woa.shape), const(wob.shape), const(wout.shape), const((1, D))],
        out_specs=rows(D),
        out_shape=jax.ShapeDtypeStruct((B, S, D), x.dtype),
        compiler_params=pltpu.CompilerParams(
            dimension_semantics=("parallel", "parallel"), vmem_limit_bytes=VMEM_LIMIT),
        name="outproj",
    )(x, mod, o_mla, o_mb, g_a, g_b, woa, wob, wout, g_post)


def kernel(x, c, positions, w_ada, b_ada, g_pre, g_post, w_in, g_q_lat, w_uq, g_kv_lat, w_ukv,
           w_o_mla, w_o_moba, b_merge, w_out):
    B, S, D = x.shape
    depth = w_ada.shape[0]
    assert S % TOKEN_TILE == 0 and S % ROPE_TILE == 0 and TOKEN_TILE % ATTN_TILE == 0
    assert w_in.shape[2] == (MLA_Q_RANK + MLA_KV_RANK + MLA_ROPE + 5 * 512 + 2 * D)
    tabs = _rope_tables(positions)
    for l in range(depth):
        mod = _adaln(c, w_ada[l], b_ada[l]).reshape(B, 3, D)
        w, wuq, wuk, wuv = _prep_in_weights(w_in[l], w_uq[l], w_ukv[l])
        (q_ml, k_ml, vt_ml, z_ml, q_mb, k_mb, vt_mb, z_mb, g_a, g_b, kmean) = _inproj(
            x, mod, g_pre[l][None], w, g_q_lat[l][None], wuq, g_kv_lat[l][None], wuk, wuv,
            b_merge[l][None], tabs)
        o_ml = _attention(q_ml, k_ml, vt_ml, z_ml)
        o_mb = _attention(q_mb, k_mb, vt_mb, z_mb, kmean.reshape(B, S // ATTN_TILE, -1))
        x = _outproj(x, mod, o_ml, o_mb, g_a, g_b,
                     w_o_mla[l].astype(BF16), w_o_moba[l].astype(BF16), w_out[l].astype(BF16),
                     g_post[l][None])
    return x
```

```python
import functools

import jax
import jax.numpy as jnp
from jax import lax
from jax.experimental import pallas as pl
from jax.experimental.pallas import tpu as pltpu

EPS = 1e-6
ROPE_THETA = 10000.0
MLA_HEADS = 8
MLA_NOPE = 64
MLA_ROPE = 32
MLA_V = 64
MLA_Q_RANK = 256
MLA_KV_RANK = 128
MOBA_HEADS = 8
MOBA_DH = 64
MOBA_BLOCK = 256
MOBA_TOPK = 3

LANES = 128
HEAD_SLAB = LANES
ATTN_TILE = MOBA_BLOCK
Q_BLOCKS = 2
ATTN_HEADS = 4
MASK_FLOOR = -1e30
SUM_ROWS = 16
LOG2E = 1.4426950408889634
TOKEN_TILE = 512
ROPE_TILE = 512
VMEM_LIMIT = 56 * 1024 * 1024

F32 = jnp.float32
BF16 = jnp.bfloat16
NEG_INF = float("-inf")


def _sigmoid(t):
    return 1.0 / (1.0 + jnp.exp(-t))


def _silu(t):
    return t * _sigmoid(t)


def _adaln_kernel(c_ref, w_ref, b_ref, o_ref):
    c = c_ref[...]
    o_ref[...] = jnp.dot(_silu(c), w_ref[...], preferred_element_type=F32) + b_ref[...]


def _adaln(c, w_ada, b_ada):
    B, D = c.shape
    n = w_ada.shape[1] // D
    return pl.pallas_call(
        _adaln_kernel,
        grid=(n,),
        in_specs=[pl.BlockSpec((B, D), lambda j: (0, 0)),
                  pl.BlockSpec((D, D), lambda j: (0, j)),
                  pl.BlockSpec((1, D), lambda j: (0, j))],
        out_specs=pl.BlockSpec((B, D), lambda j: (0, j)),
        out_shape=jax.ShapeDtypeStruct((B, n * D), F32),
        compiler_params=pltpu.CompilerParams(vmem_limit_bytes=VMEM_LIMIT),
        name="adaln",
    )(c, w_ada, b_ada.reshape(1, -1))


def _rope_kernel(pos_ref, fmb_ref, fml_ref, cmb_ref, smb_ref, cml_ref, sml_ref):
    nb, ts = pos_ref.shape
    ones = jnp.ones((2 * MLA_ROPE, ts), F32)
    zeros64 = jnp.zeros((2 * MLA_ROPE, ts), F32)
    zeros32 = jnp.zeros((MLA_ROPE, ts), F32)
    for b in range(nb):
        pos = pos_ref[b:b + 1, :].astype(F32)
        amb = fmb_ref[...] * pos
        cb, sb = jnp.cos(amb), jnp.sin(amb)
        cmb_ref[b] = jnp.concatenate([cb, cb, cb, cb], axis=0).T
        smb_ref[b] = jnp.concatenate([-sb, sb, -sb, sb], axis=0).T
        aml = fml_ref[...] * pos
        ca, sa = jnp.cos(aml), jnp.sin(aml)
        cml_ref[b] = jnp.concatenate([ones, ca, ca, zeros32], axis=0).T
        sml_ref[b] = jnp.concatenate([zeros64, -sa, sa, zeros32], axis=0).T


def _rope_tables(positions):
    B, S = positions.shape
    ts = ROPE_TILE
    half_mb, half_ml = MOBA_DH // 2, MLA_ROPE // 2
    fmb = (ROPE_THETA ** (-jnp.arange(half_mb, dtype=F32) / half_mb)).reshape(half_mb, 1)
    fml = (ROPE_THETA ** (-jnp.arange(half_ml, dtype=F32) / half_ml)).reshape(half_ml, 1)
    tab = jax.ShapeDtypeStruct((B, S, LANES), F32)
    tab_spec = pl.BlockSpec((B, ts, LANES), lambda j: (0, j, 0))
    return pl.pallas_call(
        _rope_kernel,
        grid=(S // ts,),
        in_specs=[pl.BlockSpec((B, ts), lambda j: (0, j)),
                  pl.BlockSpec((half_mb, 1), lambda j: (0, 0)),
                  pl.BlockSpec((half_ml, 1), lambda j: (0, 0))],
        out_specs=[tab_spec] * 4,
        out_shape=[tab] * 4,
        compiler_params=pltpu.CompilerParams(vmem_limit_bytes=VMEM_LIMIT),
        name="rope_tables",
    )(positions, fmb, fml)


def _rope_slab(t, cos, sin, half):
    lane = lax.broadcasted_iota(jnp.int32, t.shape, 1)
    first = (lane & (2 * half - 1)) < half
    partner = jnp.where(first, pltpu.roll(t, LANES - half, 1), pltpu.roll(t, half, 1))
    return t * cos + partner * sin


def _rms(t, g):
    return t * lax.rsqrt(jnp.mean(t * t, axis=-1, keepdims=True) + EPS) * g


def _inproj_kernel(x_ref, mod_ref, gpre_ref, w_ref, gq_ref, wuq_ref, gkv_ref, wuk_ref, wuv_ref,
                   bm_ref, cmb_ref, smb_ref, cml_ref, sml_ref,
                   qml_ref, kml_ref, vtml_ref, zml_ref, qmb_ref, kmb_ref, vtmb_ref, zmb_ref,
                   ga_ref, gb_ref, kmean_ref):
    tm = x_ref.shape[1]
    nsub = tm // ATTN_TILE
    x = x_ref[0]
    shift = mod_ref[0, 0:1, :]
    scale = mod_ref[0, 1:2, :]
    h = (_rms(x, gpre_ref[...]) * (1.0 + scale) + shift).astype(BF16)

    def proj(lo, hi):
        return jnp.dot(h, w_ref[:, lo:hi], preferred_element_type=F32)

    cmb, smb = cmb_ref[0], smb_ref[0]
    cml, sml = cml_ref[0], sml_ref[0]

    g0 = proj(0, 512)
    q_scale = float((MLA_NOPE + MLA_ROPE) ** -0.5) * LOG2E
    qn = _rms(g0[:, :MLA_Q_RANK], gq_ref[...] * q_scale).astype(BF16)
    kvn = _rms(g0[:, MLA_Q_RANK:MLA_Q_RANK + MLA_KV_RANK], gkv_ref[...]).astype(BF16)
    kpe = _rope_slab(g0[:, 384:512], cml, sml, MLA_ROPE // 2)
    q = jnp.dot(qn, wuq_ref[...], preferred_element_type=F32)
    kn = jnp.dot(kvn, wuk_ref[...], preferred_element_type=F32)
    for hd in range(MLA_HEADS):
        sl = slice(hd * HEAD_SLAB, (hd + 1) * HEAD_SLAB)
        qml_ref[0, :, sl] = _rope_slab(q[:, sl], cml, sml, MLA_ROPE // 2).astype(BF16)
        kml_ref[0, :, sl] = (kn[:, sl] + kpe).astype(BF16)
    v = jnp.dot(kvn, wuv_ref[...], preferred_element_type=F32)
    for r in range(nsub):
        vtml_ref[0, r] = v[r * ATTN_TILE:(r + 1) * ATTN_TILE, :].T.astype(BF16)
    zml_ref[0] = _silu(proj(512, 1024)).astype(BF16)

    qm = proj(1024, 1536)
    km = proj(1536, 2048)
    mb_scale = float(MOBA_DH ** -0.5) * LOG2E
    for cidx in range(MOBA_HEADS * MOBA_DH // LANES):
        sl = slice(cidx * LANES, (cidx + 1) * LANES)
        qmb_ref[0, :, sl] = (_rope_slab(qm[:, sl], cmb, smb, MOBA_DH // 2) * mb_scale).astype(BF16)
        kr = _rope_slab(km[:, sl], cmb, smb, MOBA_DH // 2)
        kmb_ref[0, :, sl] = kr.astype(BF16)
        for r in range(nsub):
            kmean_ref[0, r, :, sl] = jnp.mean(kr[r * ATTN_TILE:(r + 1) * ATTN_TILE, :], axis=0, keepdims=True)
    vm = proj(2048, 2560)
    for r in range(nsub):
        vtmb_ref[0, r] = vm[r * ATTN_TILE:(r + 1) * ATTN_TILE, :].T.astype(BF16)
    zmb_ref[0] = _silu(proj(2560, 3072)).astype(BF16)

    D = x_ref.shape[2]
    ga_ref[0] = _sigmoid(proj(3072, 3072 + D) + bm_ref[:, :D]).astype(BF16)
    gb_ref[0] = _sigmoid(proj(3072 + D, 3072 + 2 * D) + bm_ref[:, D:]).astype(BF16)


def _prep_in_weights(w_in, w_uq, w_ukv):
    D = w_in.shape[0]
    o = 0
    cols = {}
    for name, n in (("q_lat", MLA_Q_RANK), ("kv_lat", MLA_KV_RANK), ("k_rope", MLA_ROPE),
                    ("z_mla", 512), ("q_mb", 512), ("k_mb", 512), ("v_mb", 512), ("z_mb", 512),
                    ("merge", 2 * D)):
        cols[name] = w_in[:, o:o + n]
        o += n
    zpad = lambda n: jnp.zeros((D, n), w_in.dtype)
    w = jnp.concatenate([cols["q_lat"], cols["kv_lat"],
                         zpad(MLA_NOPE), cols["k_rope"], zpad(HEAD_SLAB - MLA_NOPE - MLA_ROPE),
                         cols["z_mla"], cols["q_mb"], cols["k_mb"], cols["v_mb"], cols["z_mb"],
                         cols["merge"]], axis=1).astype(BF16)
    dq = MLA_NOPE + MLA_ROPE
    wuq = w_uq.reshape(MLA_Q_RANK, MLA_HEADS, dq)
    wuq = jnp.pad(wuq, ((0, 0), (0, 0), (0, HEAD_SLAB - dq))).reshape(MLA_Q_RANK, MLA_HEADS * HEAD_SLAB)
    wukv = w_ukv.reshape(MLA_KV_RANK, MLA_HEADS, MLA_NOPE + MLA_V)
    wuk = jnp.pad(wukv[:, :, :MLA_NOPE], ((0, 0), (0, 0), (0, HEAD_SLAB - MLA_NOPE)))
    wuk = wuk.reshape(MLA_KV_RANK, MLA_HEADS * HEAD_SLAB)
    wuv = wukv[:, :, MLA_NOPE:].reshape(MLA_KV_RANK, MLA_HEADS * MLA_V)
    return w, wuq.astype(BF16), wuk.astype(BF16), wuv.astype(BF16)


def _inproj(x, mod, g_pre, w, g_q, wuq, g_kv, wuk, wuv, b_merge, tabs):
    B, S, D = x.shape
    tm = TOKEN_TILE
    nkb = S // ATTN_TILE
    nsub = tm // ATTN_TILE
    wide = MLA_HEADS * HEAD_SLAB
    mbw = MOBA_HEADS * MOBA_DH

    def const(shape):
        return pl.BlockSpec(shape, lambda b, i: (0,) * len(shape), pipeline_mode=pl.Buffered(1))

    def rows(n):
        return pl.BlockSpec((1, tm, n), lambda b, i: (b, i, 0))

    vt_spec = pl.BlockSpec((1, nsub, mbw, ATTN_TILE), lambda b, i: (b, i, 0, 0))
    bf = lambda *s: jax.ShapeDtypeStruct(s, BF16)
    out_shape = [bf(B, S, wide), bf(B, S, wide), bf(B, nkb, mbw, ATTN_TILE), bf(B, S, mbw),
                 bf(B, S, mbw), bf(B, S, mbw), bf(B, nkb, mbw, ATTN_TILE), bf(B, S, mbw),
                 bf(B, S, D), bf(B, S, D),
                 jax.ShapeDtypeStruct((B, nkb, 1, mbw), F32)]
    out_specs = [rows(wide), rows(wide), vt_spec, rows(mbw),
                 rows(mbw), rows(mbw), vt_spec, rows(mbw),
                 rows(D), rows(D),
                 pl.BlockSpec((1, nsub, 1, mbw), lambda b, i: (b, i, 0, 0))]
    in_specs = [rows(D),
                pl.BlockSpec((1, 3, D), lambda b, i: (b, 0, 0)),
                const((1, D)), const(w.shape), const((1, MLA_Q_RANK)), const(wuq.shape),
                const((1, MLA_KV_RANK)), const(wuk.shape), const(wuv.shape), const((1, 2 * D)),
                rows(LANES), rows(LANES), rows(LANES), rows(LANES)]
    return pl.pallas_call(
        _inproj_kernel,
        grid=(B, S // tm),
        in_specs=in_specs,
        out_specs=out_specs,
        out_shape=out_shape,
        compiler_params=pltpu.CompilerParams(
            dimension_semantics=("parallel", "parallel"), vmem_limit_bytes=VMEM_LIMIT),
        name="inproj",
    )(x, mod, g_pre, w, g_q, wuq, g_kv, wuk, wuv, b_merge, *tabs)


def _attn_kernel(*refs, moba):
    if moba:
        (q_ref, k_ref, vt_ref, z_ref, kmean_ref, o_ref,
         s0, s1, s2, p0, p1, cm0, cm1, cm2, al0, al1, m_ref, acc_ref, bias_ref, qm_ref) = refs
    else:
        (q_ref, k_ref, vt_ref, z_ref, o_ref,
         s0, s1, s2, p0, p1, cm0, cm1, cm2, al0, al1, m_ref, acc_ref) = refs
    t = ATTN_TILE
    tq = Q_BLOCKS * t
    nh = s0.shape[0]
    a = pl.program_id(2)
    own0 = Q_BLOCKS * a
    dv = vt_ref.shape[2] // nh
    nkb = vt_ref.shape[1]
    qcol = lax.broadcasted_iota(jnp.int32, (1, tq), 1)
    row = lax.broadcasted_iota(jnp.int32, (t, tq), 0)
    col = lax.broadcasted_iota(jnp.int32, (t, tq), 1)

    def q_head(h, first_row=0):
        if moba:
            return qm_ref[h, first_row:, :]
        return q_ref[0, first_row:, h * HEAD_SLAB:(h + 1) * HEAD_SLAB]

    def k_block(kb, h):
        start = pl.multiple_of(kb * t, t)
        lanes = (h // 2) * LANES if moba else h * HEAD_SLAB
        return k_ref[0, pl.ds(start, t), lanes:lanes + LANES]

    if moba:
        lane = lax.broadcasted_iota(jnp.int32, (tq, LANES), 1)
        for h in range(nh):
            q2 = q_ref[0, :, (h // 2) * LANES:(h // 2 + 1) * LANES]
            keep = (lane < MOBA_DH) if h % 2 == 0 else (lane >= MOBA_DH)
            qm_ref[h] = jnp.where(keep, q2, jnp.zeros_like(q2))

    def select_blocks(h):
        blk = lax.broadcasted_iota(jnp.int32, (nkb, tq), 0).astype(F32)
        own = (own0 + jnp.where(qcol >= t, 1, 0)).astype(F32)
        km = kmean_ref[0, :, (h // 2) * LANES:(h // 2 + 1) * LANES]
        g = lax.dot_general(km, qm_ref[h].astype(F32), (((1,), (1,)), ((), ())),
                            preferred_element_type=F32)
        g = jnp.where(blk < own, g, NEG_INF)
        bias = jnp.full((nkb, tq), NEG_INF, F32)
        for _ in range(MOBA_TOPK):
            best = jnp.max(g, axis=0, keepdims=True)
            hit = (g == best) & (best > NEG_INF)
            first = jnp.min(jnp.where(hit, blk, float(nkb)), axis=0, keepdims=True)
            pick = blk == first
            bias = jnp.where(pick, 0.0, bias)
            g = jnp.where(pick, NEG_INF, g)
        bias_ref[h] = jnp.where(blk == own, 0.0, bias)

    def stage_a(kb, s_ref, cm_ref, gated=moba):
        for h in range(nh):
            s = lax.dot_general(k_block(kb, h), q_head(h), (((1,), (1,)), ((), ())),
                                preferred_element_type=F32)
            s_ref[h] = s
            cm = jnp.max(s, axis=0, keepdims=True)
            if gated:
                cm = cm + bias_ref[h, pl.ds(kb, 1), :]
            cm_ref[h] = cm

    def stage_b(s_ref, cm_ref, p_ref, al_ref, kb=None):
        for h in range(nh):
            m_old = m_ref[h]
            m_new = jnp.maximum(m_old, cm_ref[h])
            m_ref[h] = m_new
            al_ref[h] = jnp.exp2(m_old - m_new)
            if moba and kb is not None:
                m_new = m_new - bias_ref[h, pl.ds(kb, 1), :]
            p_ref[h] = jnp.exp2(s_ref[h] - m_new).astype(BF16)

    ones_rows = jnp.ones((SUM_ROWS, t), BF16)

    def stage_c(kb, p_ref, al_ref, first=False):
        for h in range(nh):
            vt = jnp.concatenate([vt_ref[0, kb, h * dv:(h + 1) * dv, :], ones_rows], axis=0)
            pv = jnp.dot(vt, p_ref[h], preferred_element_type=F32)
            acc_ref[h] = pv if first else al_ref[h] * acc_ref[h] + pv

    m_ref[...] = jnp.full(m_ref.shape, MASK_FLOOR, F32)

    for h in range(nh):
        s = lax.dot_general(k_block(0, h), q_head(h), (((1,), (1,)), ((), ())),
                            preferred_element_type=F32)
        s0[h] = s
        cm = jnp.max(s, axis=0, keepdims=True)
        if moba:
            select_blocks(h)
            cm = cm + bias_ref[h, 0:1, :]
        cm0[h] = cm
    tri = jnp.where(row <= col, 0.0, NEG_INF).astype(F32)[:, :t]
    for h in range(nh):
        s = lax.dot_general(k_block(own0, h), q_head(h), (((1,), (1,)), ((), ())),
                            preferred_element_type=F32)
        s = jnp.concatenate([s[:, :t] + tri, s[:, t:]], axis=1)
        s2[h] = s
        cm = jnp.max(s, axis=0, keepdims=True)
        if moba:
            cm = cm + bias_ref[h, pl.ds(own0, 1), :]
        cm2[h] = cm
    for h in range(nh):
        s = lax.dot_general(k_block(own0 + 1, h), q_head(h, t), (((1,), (1,)), ((), ())),
                            preferred_element_type=F32) + tri
        s1[h, :, t:] = s
        cm1[h, :, t:] = jnp.max(s, axis=0, keepdims=True)
    stage_b(s2, cm2, p0, al0, own0)
    for h in range(nh):
        m_old = m_ref[h, :, t:]
        m_new = jnp.maximum(m_old, cm1[h, :, t:])
        m_ref[h, :, t:] = m_new
        al1[h] = jnp.concatenate([jnp.ones((1, t), F32), jnp.exp2(m_old - m_new)], axis=1)
        p1[h, :, :t] = jnp.zeros((t, t), BF16)
        p1[h, :, t:] = jnp.exp2(s1[h, :, t:] - m_new).astype(BF16)
    stage_c(own0, p0, al0, first=True)

    def pair(up, carry):
        kb = Q_BLOCKS * up
        stage_a(kb + 1, s1, cm1)
        stage_b(s0, cm0, p0, al0, kb)
        stage_c(jnp.where(up == 0, own0 + 1, kb - 1), p1, al1)
        stage_a(kb + 2, s0, cm0)
        stage_b(s1, cm1, p1, al1, kb + 1)
        stage_c(kb, p0, al0)
        return carry

    lax.fori_loop(0, a, pair, 0)
    stage_c(jnp.where(a == 0, own0 + 1, own0 - 1), p1, al1)

    outs = []
    for h in range(nh):
        acc = acc_ref[h]
        outs.append(acc[:dv] * (1.0 / acc[dv:dv + 1]))
    o_ref[0] = (jnp.concatenate(outs, axis=0).T * z_ref[0].astype(F32)).astype(o_ref.dtype)


def _attention(q, k, vt, z, kmean=None):
    B, S, _ = q.shape
    t = ATTN_TILE
    tq = Q_BLOCKS * t
    nh = ATTN_HEADS
    nkb = S // t
    groups = vt.shape[2] // (nh * MLA_V)
    wq = q.shape[2] // groups
    dv = vt.shape[2] // (groups * nh)
    moba = kmean is not None
    assert S % tq == 0 and dv == MLA_V == MOBA_DH
    in_specs = [pl.BlockSpec((1, tq, wq), lambda b, g, i: (b, i, g)),
                pl.BlockSpec((1, S, wq), lambda b, g, i: (b, 0, g)),
                pl.BlockSpec((1, nkb, nh * dv, t), lambda b, g, i: (b, 0, g, 0)),
                pl.BlockSpec((1, tq, nh * dv), lambda b, g, i: (b, i, g))]
    args = [q, k, vt, z]
    row_vec = pltpu.VMEM((nh, 1, tq), F32)
    score_slot = pltpu.VMEM((nh, t, tq), F32)
    scratch = [score_slot, score_slot, score_slot,
               pltpu.VMEM((nh, t, tq), BF16), pltpu.VMEM((nh, t, tq), BF16),
               row_vec, row_vec, row_vec, row_vec, row_vec,
               row_vec,
               pltpu.VMEM((nh, dv + SUM_ROWS, tq), F32)]
    if moba:
        in_specs.append(pl.BlockSpec((1, nkb, wq), lambda b, g, i: (b, 0, g)))
        args.append(kmean)
        scratch += [pltpu.VMEM((nh, nkb, tq), F32),
                    pltpu.VMEM((nh, tq, LANES), BF16)]
    return pl.pallas_call(
        functools.partial(_attn_kernel, moba=moba),
        grid=(B, groups, S // tq),
        in_specs=in_specs,
        out_specs=pl.BlockSpec((1, tq, nh * dv), lambda b, g, i: (b, i, g)),
        out_shape=jax.ShapeDtypeStruct((B, S, groups * nh * dv), BF16),
        scratch_shapes=scratch,
        compiler_params=pltpu.CompilerParams(
            dimension_semantics=("parallel", "parallel", "parallel"), vmem_limit_bytes=VMEM_LIMIT),
        name="moba_attn" if moba else "mla_attn",
    )(*args)


def _outproj_kernel(x_ref, mod_ref, oml_ref, omb_ref, ga_ref, gb_ref,
                    woa_ref, wob_ref, wout_ref, gpost_ref, o_ref):
    y_a = jnp.dot(oml_ref[0], woa_ref[...], preferred_element_type=F32)
    y_b = jnp.dot(omb_ref[0], wob_ref[...], preferred_element_type=F32)
    u = ga_ref[0].astype(F32) * y_a + gb_ref[0].astype(F32) * y_b
    y = jnp.dot(u.astype(BF16), wout_ref[...], preferred_element_type=F32)
    gate = mod_ref[0, 2:3, :]
    o_ref[0] = x_ref[0] + gate * _rms(y, gpost_ref[...])


def _outproj(x, mod, o_mla, o_mb, g_a, g_b, woa, wob, wout, g_post):
    B, S, D = x.shape
    tm = TOKEN_TILE
    width = o_mla.shape[2]

    def const(shape):
        return pl.BlockSpec(shape, lambda b, i: (0,) * len(shape), pipeline_mode=pl.Buffered(1))

    def rows(n):
        return pl.BlockSpec((1, tm, n), lambda b, i: (b, i, 0))

    return pl.pallas_call(
        _outproj_kernel,
        grid=(B, S // tm),
        in_specs=[rows(D), pl.BlockSpec((1, 3, D), lambda b, i: (b, 0, 0)),
                  rows(width), rows(width), rows(D), rows(D),
                  const(woa.shape), const(wob.shape), const(wout.shape), const((1, D))],
        out_specs=rows(D),
        out_shape=jax.ShapeDtypeStruct((B, S, D), x.dtype),
        compiler_params=pltpu.CompilerParams(
            dimension_semantics=("parallel", "parallel"), vmem_limit_bytes=VMEM_LIMIT),
        name="outproj",
    )(x, mod, o_mla, o_mb, g_a, g_b, woa, wob, wout, g_post)


def kernel(x, c, positions, w_ada, b_ada, g_pre, g_post, w_in, g_q_lat, w_uq, g_kv_lat, w_ukv,
           w_o_mla, w_o_moba, b_merge, w_out):
    B, S, D = x.shape
    depth = w_ada.shape[0]
    assert S % TOKEN_TILE == 0 and S % ROPE_TILE == 0 and TOKEN_TILE % ATTN_TILE == 0
    assert w_in.shape[2] == (MLA_Q_RANK + MLA_KV_RANK + MLA_ROPE + 5 * 512 + 2 * D)
    tabs = _rope_tables(positions)
    for l in range(depth):
        mod = _adaln(c, w_ada[l], b_ada[l]).reshape(B, 3, D)
        w, wuq, wuk, wuv = _prep_in_weights(w_in[l], w_uq[l], w_ukv[l])
        (q_ml, k_ml, vt_ml, z_ml, q_mb, k_mb, vt_mb, z_mb, g_a, g_b, kmean) = _inproj(
            x, mod, g_pre[l][None], w, g_q_lat[l][None], wuq, g_kv_lat[l][None], wuk, wuv,
            b_merge[l][None], tabs)
        o_ml = _attention(q_ml, k_ml, vt_ml, z_ml)
        o_mb = _attention(q_mb, k_mb, vt_mb, z_mb, kmean.reshape(B, S // ATTN_TILE, -1))
        x = _outproj(x, mod, o_ml, o_mb, g_a, g_b,
                     w_o_mla[l].astype(BF16), w_o_moba[l].astype(BF16), w_out[l].astype(BF16),
                     g_post[l][None])
    return x
```

```python
import functools

import jax
import jax.numpy as jnp
from jax import lax
from jax.experimental import pallas as pl
from jax.experimental.pallas import tpu as pltpu

EPS = 1e-6
ROPE_THETA = 10000.0
MLA_HEADS = 8
MLA_NOPE = 64
MLA_ROPE = 32
MLA_V = 64
MLA_Q_RANK = 256
MLA_KV_RANK = 128
MOBA_HEADS = 8
MOBA_DH = 64
MOBA_BLOCK = 256
MOBA_TOPK = 3

LANES = 128
HEAD_SLAB = LANES
ATTN_TILE = MOBA_BLOCK
Q_BLOCKS = 2
ATTN_HEADS = 4
Q_TILES = 4
MASK_FLOOR = -1e30
SUM_ROWS = 16
LOG2E = 1.4426950408889634
TOKEN_TILE = 512
ROPE_TILE = 512
VMEM_LIMIT = 56 * 1024 * 1024

F32 = jnp.float32
BF16 = jnp.bfloat16
NEG_INF = float("-inf")


def _sigmoid(t):
    return 1.0 / (1.0 + jnp.exp(-t))


def _silu(t):
    return t * _sigmoid(t)


def _adaln_kernel(c_ref, w_ref, b_ref, o_ref):
    c = c_ref[...]
    o_ref[...] = jnp.dot(_silu(c), w_ref[...], preferred_element_type=F32) + b_ref[...]


def _adaln(c, w_ada, b_ada):
    B, D = c.shape
    n = w_ada.shape[1] // D
    return pl.pallas_call(
        _adaln_kernel,
        grid=(n,),
        in_specs=[pl.BlockSpec((B, D), lambda j: (0, 0)),
                  pl.BlockSpec((D, D), lambda j: (0, j)),
                  pl.BlockSpec((1, D), lambda j: (0, j))],
        out_specs=pl.BlockSpec((B, D), lambda j: (0, j)),
        out_shape=jax.ShapeDtypeStruct((B, n * D), F32),
        compiler_params=pltpu.CompilerParams(vmem_limit_bytes=VMEM_LIMIT),
        name="adaln",
    )(c, w_ada, b_ada.reshape(1, -1))


def _rope_kernel(pos_ref, fmb_ref, fml_ref, cmb_ref, smb_ref, cml_ref, sml_ref):
    nb, ts = pos_ref.shape
    ones = jnp.ones((2 * MLA_ROPE, ts), F32)
    zeros64 = jnp.zeros((2 * MLA_ROPE, ts), F32)
    zeros32 = jnp.zeros((MLA_ROPE, ts), F32)
    for b in range(nb):
        pos = pos_ref[b:b + 1, :].astype(F32)
        amb = fmb_ref[...] * pos
        cb, sb = jnp.cos(amb), jnp.sin(amb)
        cmb_ref[b] = jnp.concatenate([cb, cb, cb, cb], axis=0).T
        smb_ref[b] = jnp.concatenate([-sb, sb, -sb, sb], axis=0).T
        aml = fml_ref[...] * pos
        ca, sa = jnp.cos(aml), jnp.sin(aml)
        cml_ref[b] = jnp.concatenate([ones, ca, ca, zeros32], axis=0).T
        sml_ref[b] = jnp.concatenate([zeros64, -sa, sa, zeros32], axis=0).T


def _rope_tables(positions):
    B, S = positions.shape
    ts = ROPE_TILE
    half_mb, half_ml = MOBA_DH // 2, MLA_ROPE // 2
    fmb = (ROPE_THETA ** (-jnp.arange(half_mb, dtype=F32) / half_mb)).reshape(half_mb, 1)
    fml = (ROPE_THETA ** (-jnp.arange(half_ml, dtype=F32) / half_ml)).reshape(half_ml, 1)
    tab = jax.ShapeDtypeStruct((B, S, LANES), F32)
    tab_spec = pl.BlockSpec((B, ts, LANES), lambda j: (0, j, 0))
    return pl.pallas_call(
        _rope_kernel,
        grid=(S // ts,),
        in_specs=[pl.BlockSpec((B, ts), lambda j: (0, j)),
                  pl.BlockSpec((half_mb, 1), lambda j: (0, 0)),
                  pl.BlockSpec((half_ml, 1), lambda j: (0, 0))],
        out_specs=[tab_spec] * 4,
        out_shape=[tab] * 4,
        compiler_params=pltpu.CompilerParams(vmem_limit_bytes=VMEM_LIMIT),
        name="rope_tables",
    )(positions, fmb, fml)


def _rope_slab(t, cos, sin, half):
    lane = lax.broadcasted_iota(jnp.int32, t.shape, 1)
    first = (lane & (2 * half - 1)) < half
    partner = jnp.where(first, pltpu.roll(t, LANES - half, 1), pltpu.roll(t, half, 1))
    return t * cos + partner * sin


def _rms(t, g):
    return t * lax.rsqrt(jnp.mean(t * t, axis=-1, keepdims=True) + EPS) * g


def _inproj_kernel(x_ref, mod_ref, gpre_ref, w_ref, gq_ref, wuq_ref, gkv_ref, wuk_ref, wuv_ref,
                   bm_ref, cmb_ref, smb_ref, cml_ref, sml_ref,
                   qml_ref, kml_ref, vtml_ref, zml_ref, qmb_ref, kmb_ref, vtmb_ref, zmb_ref,
                   ga_ref, gb_ref, kmean_ref):
    tm = x_ref.shape[1]
    nsub = tm // ATTN_TILE
    x = x_ref[0]
    shift = mod_ref[0, 0:1, :]
    scale = mod_ref[0, 1:2, :]
    h = (_rms(x, gpre_ref[...]) * (1.0 + scale) + shift).astype(BF16)

    def proj(lo, hi):
        return jnp.dot(h, w_ref[:, lo:hi], preferred_element_type=F32)

    cmb, smb = cmb_ref[0], smb_ref[0]
    cml, sml = cml_ref[0], sml_ref[0]

    g0 = proj(0, 512)
    q_scale = float((MLA_NOPE + MLA_ROPE) ** -0.5) * LOG2E
    qn = _rms(g0[:, :MLA_Q_RANK], gq_ref[...] * q_scale).astype(BF16)
    kvn = _rms(g0[:, MLA_Q_RANK:MLA_Q_RANK + MLA_KV_RANK], gkv_ref[...]).astype(BF16)
    kpe = _rope_slab(g0[:, 384:512], cml, sml, MLA_ROPE // 2)
    q = jnp.dot(qn, wuq_ref[...], preferred_element_type=F32)
    kn = jnp.dot(kvn, wuk_ref[...], preferred_element_type=F32)
    for hd in range(MLA_HEADS):
        sl = slice(hd * HEAD_SLAB, (hd + 1) * HEAD_SLAB)
        qml_ref[0, :, sl] = _rope_slab(q[:, sl], cml, sml, MLA_ROPE // 2).astype(BF16)
        kml_ref[0, :, sl] = (kn[:, sl] + kpe).astype(BF16)
    v = jnp.dot(kvn, wuv_ref[...], preferred_element_type=F32)
    for r in range(nsub):
        vtml_ref[0, r] = v[r * ATTN_TILE:(r + 1) * ATTN_TILE, :].T.astype(BF16)
    zml_ref[0] = _silu(proj(512, 1024)).astype(BF16)

    qm = proj(1024, 1536)
    km = proj(1536, 2048)
    mb_scale = float(MOBA_DH ** -0.5) * LOG2E
    for cidx in range(MOBA_HEADS * MOBA_DH // LANES):
        sl = slice(cidx * LANES, (cidx + 1) * LANES)
        qmb_ref[0, :, sl] = (_rope_slab(qm[:, sl], cmb, smb, MOBA_DH // 2) * mb_scale).astype(BF16)
        kr = _rope_slab(km[:, sl], cmb, smb, MOBA_DH // 2)
        kmb_ref[0, :, sl] = kr.astype(BF16)
        for r in range(nsub):
            kmean_ref[0, r, :, sl] = jnp.mean(kr[r * ATTN_TILE:(r + 1) * ATTN_TILE, :], axis=0, keepdims=True)
    vm = proj(2048, 2560)
    for r in range(nsub):
        vtmb_ref[0, r] = vm[r * ATTN_TILE:(r + 1) * ATTN_TILE, :].T.astype(BF16)
    zmb_ref[0] = _silu(proj(2560, 3072)).astype(BF16)

    D = x_ref.shape[2]
    ga_ref[0] = _sigmoid(proj(3072, 3072 + D) + bm_ref[:, :D]).astype(BF16)
    gb_ref[0] = _sigmoid(proj(3072 + D, 3072 + 2 * D) + bm_ref[:, D:]).astype(BF16)


def _prep_in_weights(w_in, w_uq, w_ukv):
    D = w_in.shape[0]
    o = 0
    cols = {}
    for name, n in (("q_lat", MLA_Q_RANK), ("kv_lat", MLA_KV_RANK), ("k_rope", MLA_ROPE),
                    ("z_mla", 512), ("q_mb", 512), ("k_mb", 512), ("v_mb", 512), ("z_mb", 512),
                    ("merge", 2 * D)):
        cols[name] = w_in[:, o:o + n]
        o += n
    zpad = lambda n: jnp.zeros((D, n), w_in.dtype)
    w = jnp.concatenate([cols["q_lat"], cols["kv_lat"],
                         zpad(MLA_NOPE), cols["k_rope"], zpad(HEAD_SLAB - MLA_NOPE - MLA_ROPE),
                         cols["z_mla"], cols["q_mb"], cols["k_mb"], cols["v_mb"], cols["z_mb"],
                         cols["merge"]], axis=1).astype(BF16)
    dq = MLA_NOPE + MLA_ROPE
    wuq = w_uq.reshape(MLA_Q_RANK, MLA_HEADS, dq)
    wuq = jnp.pad(wuq, ((0, 0), (0, 0), (0, HEAD_SLAB - dq))).reshape(MLA_Q_RANK, MLA_HEADS * HEAD_SLAB)
    wukv = w_ukv.reshape(MLA_KV_RANK, MLA_HEADS, MLA_NOPE + MLA_V)
    wuk = jnp.pad(wukv[:, :, :MLA_NOPE], ((0, 0), (0, 0), (0, HEAD_SLAB - MLA_NOPE)))
    wuk = wuk.reshape(MLA_KV_RANK, MLA_HEADS * HEAD_SLAB)
    wuv = wukv[:, :, MLA_NOPE:].reshape(MLA_KV_RANK, MLA_HEADS * MLA_V)
    return w, wuq.astype(BF16), wuk.astype(BF16), wuv.astype(BF16)


def _inproj(x, mod, g_pre, w, g_q, wuq, g_kv, wuk, wuv, b_merge, tabs):
    B, S, D = x.shape
    tm = TOKEN_TILE
    nkb = S // ATTN_TILE
    nsub = tm // ATTN_TILE
    wide = MLA_HEADS * HEAD_SLAB
    mbw = MOBA_HEADS * MOBA_DH

    def const(shape):
        return pl.BlockSpec(shape, lambda b, i: (0,) * len(shape), pipeline_mode=pl.Buffered(1))

    def rows(n):
        return pl.BlockSpec((1, tm, n), lambda b, i: (b, i, 0))

    vt_spec = pl.BlockSpec((1, nsub, mbw, ATTN_TILE), lambda b, i: (b, i, 0, 0))
    bf = lambda *s: jax.ShapeDtypeStruct(s, BF16)
    out_shape = [bf(B, S, wide), bf(B, S, wide), bf(B, nkb, mbw, ATTN_TILE), bf(B, S, mbw),
                 bf(B, S, mbw), bf(B, S, mbw), bf(B, nkb, mbw, ATTN_TILE), bf(B, S, mbw),
                 bf(B, S, D), bf(B, S, D),
                 jax.ShapeDtypeStruct((B, nkb, 1, mbw), F32)]
    out_specs = [rows(wide), rows(wide), vt_spec, rows(mbw),
                 rows(mbw), rows(mbw), vt_spec, rows(mbw),
                 rows(D), rows(D),
                 pl.BlockSpec((1, nsub, 1, mbw), lambda b, i: (b, i, 0, 0))]
    in_specs = [rows(D),
                pl.BlockSpec((1, 3, D), lambda b, i: (b, 0, 0)),
                const((1, D)), const(w.shape), const((1, MLA_Q_RANK)), const(wuq.shape),
                const((1, MLA_KV_RANK)), const(wuk.shape), const(wuv.shape), const((1, 2 * D)),
                rows(LANES), rows(LANES), rows(LANES), rows(LANES)]
    return pl.pallas_call(
        _inproj_kernel,
        grid=(B, S // tm),
        in_specs=in_specs,
        out_specs=out_specs,
        out_shape=out_shape,
        compiler_params=pltpu.CompilerParams(
            dimension_semantics=("parallel", "parallel"), vmem_limit_bytes=VMEM_LIMIT),
        name="inproj",
    )(x, mod, g_pre, w, g_q, wuq, g_kv, wuk, wuv, b_merge, *tabs)


def _attn_kernel(*refs, moba):
    def tile(qt, carry):
        _attn_tile(qt, refs, moba)
        return carry

    lax.fori_loop(0, Q_TILES, tile, 0)


def _attn_tile(qt, refs, moba):
    if moba:
        (q_ref, k_ref, vt_ref, z_ref, kmean_ref, o_ref,
         s0, s1, s2, p0, p1, cm0, cm1, cm2, al0, al1, m_ref, acc_ref, bias_ref, qm_ref) = refs
    else:
        (q_ref, k_ref, vt_ref, z_ref, o_ref,
         s0, s1, s2, p0, p1, cm0, cm1, cm2, al0, al1, m_ref, acc_ref) = refs
    t = ATTN_TILE
    tq = Q_BLOCKS * t
    nh = s0.shape[0]
    a = pl.program_id(2) * Q_TILES + qt
    q_rows = pl.ds(pl.multiple_of(qt * tq, tq), tq)
    own0 = Q_BLOCKS * a
    dv = vt_ref.shape[2] // nh
    nkb = vt_ref.shape[1]
    qcol = lax.broadcasted_iota(jnp.int32, (1, tq), 1)
    row = lax.broadcasted_iota(jnp.int32, (t, tq), 0)
    col = lax.broadcasted_iota(jnp.int32, (t, tq), 1)

    def q_head(h, first_row=0):
        if moba:
            return qm_ref[h, first_row:, :]
        rows = pl.ds(pl.multiple_of(qt * tq + first_row, t), tq - first_row)
        return q_ref[0, rows, h * HEAD_SLAB:(h + 1) * HEAD_SLAB]

    def k_block(kb, h):
        start = pl.multiple_of(kb * t, t)
        lanes = (h // 2) * LANES if moba else h * HEAD_SLAB
        return k_ref[0, pl.ds(start, t), lanes:lanes + LANES]

    if moba:
        lane = lax.broadcasted_iota(jnp.int32, (tq, LANES), 1)
        for h in range(nh):
            q2 = q_ref[0, q_rows, (h // 2) * LANES:(h // 2 + 1) * LANES]
            keep = (lane < MOBA_DH) if h % 2 == 0 else (lane >= MOBA_DH)
            qm_ref[h] = jnp.where(keep, q2, jnp.zeros_like(q2))

    def select_blocks(h):
        blk = lax.broadcasted_iota(jnp.int32, (nkb, tq), 0).astype(F32)
        own = (own0 + jnp.where(qcol >= t, 1, 0)).astype(F32)
        km = kmean_ref[0, :, (h // 2) * LANES:(h // 2 + 1) * LANES]
        g = lax.dot_general(km, qm_ref[h].astype(F32), (((1,), (1,)), ((), ())),
                            preferred_element_type=F32)
        g = jnp.where(blk < own, g, NEG_INF)
        bias = jnp.full((nkb, tq), NEG_INF, F32)
        for _ in range(MOBA_TOPK):
            best = jnp.max(g, axis=0, keepdims=True)
            hit = (g == best) & (best > NEG_INF)
            first = jnp.min(jnp.where(hit, blk, float(nkb)), axis=0, keepdims=True)
            pick = blk == first
            bias = jnp.where(pick, 0.0, bias)
            g = jnp.where(pick, NEG_INF, g)
        bias_ref[h] = jnp.where(blk == own, 0.0, bias)

    def stage_a(kb, s_ref, cm_ref, gated=moba):
        for h in range(nh):
            s = lax.dot_general(k_block(kb, h), q_head(h), (((1,), (1,)), ((), ())),
                                preferred_element_type=F32)
            s_ref[h] = s
            cm = jnp.max(s, axis=0, keepdims=True)
            if gated:
                cm = cm + bias_ref[h, pl.ds(kb, 1), :]
            cm_ref[h] = cm

    def stage_b(s_ref, cm_ref, p_ref, al_ref, kb=None):
        for h in range(nh):
            m_old = m_ref[h]
            m_new = jnp.maximum(m_old, cm_ref[h])
            m_ref[h] = m_new
            al_ref[h] = jnp.exp2(m_old - m_new)
            if moba and kb is not None:
                m_new = m_new - bias_ref[h, pl.ds(kb, 1), :]
            p_ref[h] = jnp.exp2(s_ref[h] - m_new).astype(BF16)

    ones_rows = jnp.ones((SUM_ROWS, t), BF16)

    def stage_c(kb, p_ref, al_ref, first=False):
        for h in range(nh):
            vt = jnp.concatenate([vt_ref[0, kb, h * dv:(h + 1) * dv, :], ones_rows], axis=0)
            pv = jnp.dot(vt, p_ref[h], preferred_element_type=F32)
            acc_ref[h] = pv if first else al_ref[h] * acc_ref[h] + pv

    m_ref[...] = jnp.full(m_ref.shape, MASK_FLOOR, F32)

    for h in range(nh):
        s = lax.dot_general(k_block(0, h), q_head(h), (((1,), (1,)), ((), ())),
                            preferred_element_type=F32)
        s0[h] = s
        cm = jnp.max(s, axis=0, keepdims=True)
        if moba:
            select_blocks(h)
            cm = cm + bias_ref[h, 0:1, :]
        cm0[h] = cm
    tri = jnp.where(row <= col, 0.0, NEG_INF).astype(F32)[:, :t]
    for h in range(nh):
        s = lax.dot_general(k_block(own0, h), q_head(h), (((1,), (1,)), ((), ())),
                            preferred_element_type=F32)
        s = jnp.concatenate([s[:, :t] + tri, s[:, t:]], axis=1)
        s2[h] = s
        cm = jnp.max(s, axis=0, keepdims=True)
        if moba:
            cm = cm + bias_ref[h, pl.ds(own0, 1), :]
        cm2[h] = cm
    for h in range(nh):
        s = lax.dot_general(k_block(own0 + 1, h), q_head(h, t), (((1,), (1,)), ((), ())),
                            preferred_element_type=F32) + tri
        s1[h, :, t:] = s
        cm1[h, :, t:] = jnp.max(s, axis=0, keepdims=True)
    stage_b(s2, cm2, p0, al0, own0)
    for h in range(nh):
        m_old = m_ref[h, :, t:]
        m_new = jnp.maximum(m_old, cm1[h, :, t:])
        m_ref[h, :, t:] = m_new
        al1[h] = jnp.concatenate([jnp.ones((1, t), F32), jnp.exp2(m_old - m_new)], axis=1)
        p1[h, :, :t] = jnp.zeros((t, t), BF16)
        p1[h, :, t:] = jnp.exp2(s1[h, :, t:] - m_new).astype(BF16)
    stage_c(own0, p0, al0, first=True)

    def pair(up, carry):
        kb = Q_BLOCKS * up
        stage_a(kb + 1, s1, cm1)
        stage_b(s0, cm0, p0, al0, kb)
        stage_c(jnp.where(up == 0, own0 + 1, kb - 1), p1, al1)
        stage_a(kb + 2, s0, cm0)
        stage_b(s1, cm1, p1, al1, kb + 1)
        stage_c(kb, p0, al0)
        return carry

    lax.fori_loop(0, a, pair, 0)
    stage_c(jnp.where(a == 0, own0 + 1, own0 - 1), p1, al1)

    outs = []
    for h in range(nh):
        acc = acc_ref[h]
        outs.append(acc[:dv] * (1.0 / acc[dv:dv + 1]))
    o_ref[0, q_rows, :] = (jnp.concatenate(outs, axis=0).T
                           * z_ref[0, q_rows, :].astype(F32)).astype(o_ref.dtype)


def _attention(q, k, vt, z, kmean=None):
    B, S, _ = q.shape
    t = ATTN_TILE
    tq = Q_BLOCKS * t
    nh = ATTN_HEADS
    nkb = S // t
    groups = vt.shape[2] // (nh * MLA_V)
    wq = q.shape[2] // groups
    dv = vt.shape[2] // (groups * nh)
    moba = kmean is not None
    rows = Q_TILES * tq
    assert S % rows == 0 and dv == MLA_V == MOBA_DH
    in_specs = [pl.BlockSpec((1, rows, wq), lambda b, g, i: (b, i, g)),
                pl.BlockSpec((1, S, wq), lambda b, g, i: (b, 0, g)),
                pl.BlockSpec((1, nkb, nh * dv, t), lambda b, g, i: (b, 0, g, 0)),
                pl.BlockSpec((1, rows, nh * dv), lambda b, g, i: (b, i, g))]
    args = [q, k, vt, z]
    row_vec = pltpu.VMEM((nh, 1, tq), F32)
    score_slot = pltpu.VMEM((nh, t, tq), F32)
    scratch = [score_slot, score_slot, score_slot,
               pltpu.VMEM((nh, t, tq), BF16), pltpu.VMEM((nh, t, tq), BF16),
               row_vec, row_vec, row_vec, row_vec, row_vec,
               row_vec,
               pltpu.VMEM((nh, dv + SUM_ROWS, tq), F32)]
    if moba:
        in_specs.append(pl.BlockSpec((1, nkb, wq), lambda b, g, i: (b, 0, g)))
        args.append(kmean)
        scratch += [pltpu.VMEM((nh, nkb, tq), F32),
                    pltpu.VMEM((nh, tq, LANES), BF16)]
    return pl.pallas_call(
        functools.partial(_attn_kernel, moba=moba),
        grid=(B, groups, S // rows),
        in_specs=in_specs,
        out_specs=pl.BlockSpec((1, rows, nh * dv), lambda b, g, i: (b, i, g)),
        out_shape=jax.ShapeDtypeStruct((B, S, groups * nh * dv), BF16),
        scratch_shapes=scratch,
        compiler_params=pltpu.CompilerParams(
            dimension_semantics=("parallel", "parallel", "parallel"), vmem_limit_bytes=VMEM_LIMIT),
        name="moba_attn" if moba else "mla_attn",
    )(*args)


def _outproj_kernel(x_ref, mod_ref, oml_ref, omb_ref, ga_ref, gb_ref,
                    woa_ref, wob_ref, wout_ref, gpost_ref, o_ref):
    y_a = jnp.dot(oml_ref[0], woa_ref[...], preferred_element_type=F32)
    y_b = jnp.dot(omb_ref[0], wob_ref[...], preferred_element_type=F32)
    u = ga_ref[0].astype(F32) * y_a + gb_ref[0].astype(F32) * y_b
    y = jnp.dot(u.astype(BF16), wout_ref[...], preferred_element_type=F32)
    gate = mod_ref[0, 2:3, :]
    o_ref[0] = x_ref[0] + gate * _rms(y, gpost_ref[...])


def _outproj(x, mod, o_mla, o_mb, g_a, g_b, woa, wob, wout, g_post):
    B, S, D = x.shape
    tm = TOKEN_TILE
    width = o_mla.shape[2]

    def const(shape):
        return pl.BlockSpec(shape, lambda b, i: (0,) * len(shape), pipeline_mode=pl.Buffered(1))

    def rows(n):
        return pl.BlockSpec((1, tm, n), lambda b, i: (b, i, 0))

    return pl.pallas_call(
        _outproj_kernel,
        grid=(B, S // tm),
        in_specs=[rows(D), pl.BlockSpec((1, 3, D), lambda b, i: (b, 0, 0)),
                  rows(width), rows(width), rows(D), rows(D),
                  const(woa.shape), const(wob.shape), const(wout.shape), const((1, D))],
        out_specs=rows(D),
        out_shape=jax.ShapeDtypeStruct((B, S, D), x.dtype),
        compiler_params=pltpu.CompilerParams(
            dimension_semantics=("parallel", "parallel"), vmem_limit_bytes=VMEM_LIMIT),
        name="outproj",
    )(x, mod, o_mla, o_mb, g_a, g_b, woa, wob, wout, g_post)


def kernel(x, c, positions, w_ada, b_ada, g_pre, g_post, w_in, g_q_lat, w_uq, g_kv_lat, w_ukv,
           w_o_mla, w_o_moba, b_merge, w_out):
    B, S, D = x.shape
    depth = w_ada.shape[0]
    assert S % TOKEN_TILE == 0 and S % ROPE_TILE == 0 and TOKEN_TILE % ATTN_TILE == 0
    assert w_in.shape[2] == (MLA_Q_RANK + MLA_KV_RANK + MLA_ROPE + 5 * 512 + 2 * D)
    tabs = _rope_tables(positions)
    for l in range(depth):
        mod = _adaln(c, w_ada[l], b_ada[l]).reshape(B, 3, D)
        w, wuq, wuk, wuv = _prep_in_weights(w_in[l], w_uq[l], w_ukv[l])
        (q_ml, k_ml, vt_ml, z_ml, q_mb, k_mb, vt_mb, z_mb, g_a, g_b, kmean) = _inproj(
            x, mod, g_pre[l][None], w, g_q_lat[l][None], wuq, g_kv_lat[l][None], wuk, wuv,
            b_merge[l][None], tabs)
        o_ml = _attention(q_ml, k_ml, vt_ml, z_ml)
        o_mb = _attention(q_mb, k_mb, vt_mb, z_mb, kmean.reshape(B, S // ATTN_TILE, -1))
        x = _outproj(x, mod, o_ml, o_mb, g_a, g_b,
                     w_o_mla[l].astype(BF16), w_o_moba[l].astype(BF16), w_out[l].astype(BF16),
                     g_post[l][None])
    return x
```

```python
import functools

import jax
import jax.numpy as jnp
from jax import lax
from jax.experimental import pallas as pl
from jax.experimental.pallas import tpu as pltpu

EPS = 1e-6
ROPE_THETA = 10000.0
MLA_HEADS = 8
MLA_NOPE = 64
MLA_ROPE = 32
MLA_V = 64
MLA_Q_RANK = 256
MLA_KV_RANK = 128
MOBA_HEADS = 8
MOBA_DH = 64
MOBA_BLOCK = 256
MOBA_TOPK = 3

LANES = 128
HEAD_SLAB = LANES
ATTN_TILE = MOBA_BLOCK
Q_BLOCKS = 2
ATTN_HEADS = 4
MASK_FLOOR = -1e30
SUM_ROWS = 16
LOG2E = 1.4426950408889634
TOKEN_TILE = 512
ROPE_TILE = 512
VMEM_LIMIT = 56 * 1024 * 1024

F32 = jnp.float32
BF16 = jnp.bfloat16
NEG_INF = float("-inf")


def _sigmoid(t):
    return 1.0 / (1.0 + jnp.exp(-t))


def _silu(t):
    return t * _sigmoid(t)


def _adaln_kernel(c_ref, w_ref, b_ref, o_ref):
    c = c_ref[...]
    o_ref[...] = jnp.dot(_silu(c), w_ref[...], preferred_element_type=F32) + b_ref[...]


def _adaln(c, w_ada, b_ada):
    B, D = c.shape
    n = w_ada.shape[1] // D
    return pl.pallas_call(
        _adaln_kernel,
        grid=(n,),
        in_specs=[pl.BlockSpec((B, D), lambda j: (0, 0)),
                  pl.BlockSpec((D, D), lambda j: (0, j)),
                  pl.BlockSpec((1, D), lambda j: (0, j))],
        out_specs=pl.BlockSpec((B, D), lambda j: (0, j)),
        out_shape=jax.ShapeDtypeStruct((B, n * D), F32),
        compiler_params=pltpu.CompilerParams(vmem_limit_bytes=VMEM_LIMIT),
        name="adaln",
    )(c, w_ada, b_ada.reshape(1, -1))


def _rope_kernel(pos_ref, fmb_ref, fml_ref, cmb_ref, smb_ref, cml_ref, sml_ref):
    nb, ts = pos_ref.shape
    ones = jnp.ones((2 * MLA_ROPE, ts), F32)
    zeros64 = jnp.zeros((2 * MLA_ROPE, ts), F32)
    zeros32 = jnp.zeros((MLA_ROPE, ts), F32)
    for b in range(nb):
        pos = pos_ref[b:b + 1, :].astype(F32)
        amb = fmb_ref[...] * pos
        cb, sb = jnp.cos(amb), jnp.sin(amb)
        cmb_ref[b] = jnp.concatenate([cb, cb, cb, cb], axis=0).T
        smb_ref[b] = jnp.concatenate([-sb, sb, -sb, sb], axis=0).T
        aml = fml_ref[...] * pos
        ca, sa = jnp.cos(aml), jnp.sin(aml)
        cml_ref[b] = jnp.concatenate([ones, ca, ca, zeros32], axis=0).T
        sml_ref[b] = jnp.concatenate([zeros64, -sa, sa, zeros32], axis=0).T


def _rope_tables(positions):
    B, S = positions.shape
    ts = ROPE_TILE
    half_mb, half_ml = MOBA_DH // 2, MLA_ROPE // 2
    fmb = (ROPE_THETA ** (-jnp.arange(half_mb, dtype=F32) / half_mb)).reshape(half_mb, 1)
    fml = (ROPE_THETA ** (-jnp.arange(half_ml, dtype=F32) / half_ml)).reshape(half_ml, 1)
    tab = jax.ShapeDtypeStruct((B, S, LANES), F32)
    tab_spec = pl.BlockSpec((B, ts, LANES), lambda j: (0, j, 0))
    return pl.pallas_call(
        _rope_kernel,
        grid=(S // ts,),
        in_specs=[pl.BlockSpec((B, ts), lambda j: (0, j)),
                  pl.BlockSpec((half_mb, 1), lambda j: (0, 0)),
                  pl.BlockSpec((half_ml, 1), lambda j: (0, 0))],
        out_specs=[tab_spec] * 4,
        out_shape=[tab] * 4,
        compiler_params=pltpu.CompilerParams(vmem_limit_bytes=VMEM_LIMIT),
        name="rope_tables",
    )(positions, fmb, fml)


def _rope_slab(t, cos, sin, half):
    lane = lax.broadcasted_iota(jnp.int32, t.shape, 1)
    first = (lane & (2 * half - 1)) < half
    partner = jnp.where(first, pltpu.roll(t, LANES - half, 1), pltpu.roll(t, half, 1))
    return t * cos + partner * sin


def _rms(t, g):
    return t * lax.rsqrt(jnp.mean(t * t, axis=-1, keepdims=True) + EPS) * g


def _inproj_kernel(x_ref, mod_ref, gpre_ref, w_ref, gq_ref, wuq_ref, gkv_ref, wuk_ref, wuv_ref,
                   bm_ref, cmb_ref, smb_ref, cml_ref, sml_ref,
                   qml_ref, kml_ref, vtml_ref, zml_ref, qmb_ref, kmb_ref, vtmb_ref, zmb_ref,
                   ga_ref, gb_ref, kmean_ref):
    tm = x_ref.shape[1]
    nsub = tm // ATTN_TILE
    x = x_ref[0]
    shift = mod_ref[0, 0:1, :]
    scale = mod_ref[0, 1:2, :]
    h = (_rms(x, gpre_ref[...]) * (1.0 + scale) + shift).astype(BF16)

    def proj(lo, hi):
        return jnp.dot(h, w_ref[:, lo:hi], preferred_element_type=F32)

    cmb, smb = cmb_ref[0], smb_ref[0]
    cml, sml = cml_ref[0], sml_ref[0]

    g0 = proj(0, 512)
    q_scale = float((MLA_NOPE + MLA_ROPE) ** -0.5) * LOG2E
    qn = _rms(g0[:, :MLA_Q_RANK], gq_ref[...] * q_scale).astype(BF16)
    kvn = _rms(g0[:, MLA_Q_RANK:MLA_Q_RANK + MLA_KV_RANK], gkv_ref[...]).astype(BF16)
    kpe = _rope_slab(g0[:, 384:512], cml, sml, MLA_ROPE // 2)
    q = jnp.dot(qn, wuq_ref[...], preferred_element_type=F32)
    kn = jnp.dot(kvn, wuk_ref[...], preferred_element_type=F32)
    for hd in range(MLA_HEADS):
        sl = slice(hd * HEAD_SLAB, (hd + 1) * HEAD_SLAB)
        qml_ref[0, :, sl] = _rope_slab(q[:, sl], cml, sml, MLA_ROPE // 2).astype(BF16)
        kml_ref[0, :, sl] = (kn[:, sl] + kpe).astype(BF16)
    v = jnp.dot(kvn, wuv_ref[...], preferred_element_type=F32)
    for r in range(nsub):
        vtml_ref[0, r] = v[r * ATTN_TILE:(r + 1) * ATTN_TILE, :].T.astype(BF16)
    zml_ref[0] = _silu(proj(512, 1024)).astype(BF16)

    qm = proj(1024, 1536)
    km = proj(1536, 2048)
    mb_scale = float(MOBA_DH ** -0.5) * LOG2E
    for cidx in range(MOBA_HEADS * MOBA_DH // LANES):
        sl = slice(cidx * LANES, (cidx + 1) * LANES)
        qmb_ref[0, :, sl] = (_rope_slab(qm[:, sl], cmb, smb, MOBA_DH // 2) * mb_scale).astype(BF16)
        kr = _rope_slab(km[:, sl], cmb, smb, MOBA_DH // 2)
        kmb_ref[0, :, sl] = kr.astype(BF16)
        for r in range(nsub):
            kmean_ref[0, r, :, sl] = jnp.mean(kr[r * ATTN_TILE:(r + 1) * ATTN_TILE, :], axis=0, keepdims=True)
    vm = proj(2048, 2560)
    for r in range(nsub):
        vtmb_ref[0, r] = vm[r * ATTN_TILE:(r + 1) * ATTN_TILE, :].T.astype(BF16)
    zmb_ref[0] = _silu(proj(2560, 3072)).astype(BF16)

    D = x_ref.shape[2]
    ga_ref[0] = _sigmoid(proj(3072, 3072 + D) + bm_ref[:, :D]).astype(BF16)
    gb_ref[0] = _sigmoid(proj(3072 + D, 3072 + 2 * D) + bm_ref[:, D:]).astype(BF16)


def _prep_in_weights(w_in, w_uq, w_ukv):
    D = w_in.shape[0]
    o = 0
    cols = {}
    for name, n in (("q_lat", MLA_Q_RANK), ("kv_lat", MLA_KV_RANK), ("k_rope", MLA_ROPE),
                    ("z_mla", 512), ("q_mb", 512), ("k_mb", 512), ("v_mb", 512), ("z_mb", 512),
                    ("merge", 2 * D)):
        cols[name] = w_in[:, o:o + n]
        o += n
    zpad = lambda n: jnp.zeros((D, n), w_in.dtype)
    w = jnp.concatenate([cols["q_lat"], cols["kv_lat"],
                         zpad(MLA_NOPE), cols["k_rope"], zpad(HEAD_SLAB - MLA_NOPE - MLA_ROPE),
                         cols["z_mla"], cols["q_mb"], cols["k_mb"], cols["v_mb"], cols["z_mb"],
                         cols["merge"]], axis=1).astype(BF16)
    dq = MLA_NOPE + MLA_ROPE
    wuq = w_uq.reshape(MLA_Q_RANK, MLA_HEADS, dq)
    wuq = jnp.pad(wuq, ((0, 0), (0, 0), (0, HEAD_SLAB - dq))).reshape(MLA_Q_RANK, MLA_HEADS * HEAD_SLAB)
    wukv = w_ukv.reshape(MLA_KV_RANK, MLA_HEADS, MLA_NOPE + MLA_V)
    wuk = jnp.pad(wukv[:, :, :MLA_NOPE], ((0, 0), (0, 0), (0, HEAD_SLAB - MLA_NOPE)))
    wuk = wuk.reshape(MLA_KV_RANK, MLA_HEADS * HEAD_SLAB)
    wuv = wukv[:, :, MLA_NOPE:].reshape(MLA_KV_RANK, MLA_HEADS * MLA_V)
    return w, wuq.astype(BF16), wuk.astype(BF16), wuv.astype(BF16)


def _inproj(x, mod, g_pre, w, g_q, wuq, g_kv, wuk, wuv, b_merge, tabs):
    B, S, D = x.shape
    tm = TOKEN_TILE
    nkb = S // ATTN_TILE
    nsub = tm // ATTN_TILE
    wide = MLA_HEADS * HEAD_SLAB
    mbw = MOBA_HEADS * MOBA_DH

    def const(shape):
        return pl.BlockSpec(shape, lambda b, i: (0,) * len(shape), pipeline_mode=pl.Buffered(1))

    def rows(n):
        return pl.BlockSpec((1, tm, n), lambda b, i: (b, i, 0))

    vt_spec = pl.BlockSpec((1, nsub, mbw, ATTN_TILE), lambda b, i: (b, i, 0, 0))
    bf = lambda *s: jax.ShapeDtypeStruct(s, BF16)
    out_shape = [bf(B, S, wide), bf(B, S, wide), bf(B, nkb, mbw, ATTN_TILE), bf(B, S, mbw),
                 bf(B, S, mbw), bf(B, S, mbw), bf(B, nkb, mbw, ATTN_TILE), bf(B, S, mbw),
                 bf(B, S, D), bf(B, S, D),
                 jax.ShapeDtypeStruct((B, nkb, 1, mbw), F32)]
    out_specs = [rows(wide), rows(wide), vt_spec, rows(mbw),
                 rows(mbw), rows(mbw), vt_spec, rows(mbw),
                 rows(D), rows(D),
                 pl.BlockSpec((1, nsub, 1, mbw), lambda b, i: (b, i, 0, 0))]
    in_specs = [rows(D),
                pl.BlockSpec((1, 3, D), lambda b, i: (b, 0, 0)),
                const((1, D)), const(w.shape), const((1, MLA_Q_RANK)), const(wuq.shape),
                const((1, MLA_KV_RANK)), const(wuk.shape), const(wuv.shape), const((1, 2 * D)),
                rows(LANES), rows(LANES), rows(LANES), rows(LANES)]
    return pl.pallas_call(
        _inproj_kernel,
        grid=(B, S // tm),
        in_specs=in_specs,
        out_specs=out_specs,
        out_shape=out_shape,
        compiler_params=pltpu.CompilerParams(
            dimension_semantics=("parallel", "parallel"), vmem_limit_bytes=VMEM_LIMIT),
        name="inproj",
    )(x, mod, g_pre, w, g_q, wuq, g_kv, wuk, wuv, b_merge, *tabs)


def _attn_kernel(*refs, moba):
    if moba:
        (q_ref, k_ref, vt_ref, z_ref, kmean_ref, o_ref,
         s0, s1, s2, p0, p1, cm0, cm1, cm2, al0, al1, m_ref, acc_ref, bias_ref, qm_ref) = refs
    else:
        (q_ref, k_ref, vt_ref, z_ref, o_ref,
         s0, s1, s2, p0, p1, cm0, cm1, cm2, al0, al1, m_ref, acc_ref) = refs
    t = ATTN_TILE
    tq = Q_BLOCKS * t
    nh = s0.shape[0]
    a = pl.program_id(2)
    own0 = Q_BLOCKS * a
    dv = vt_ref.shape[2] // nh
    nkb = vt_ref.shape[1]
    qcol = lax.broadcasted_iota(jnp.int32, (1, tq), 1)
    row = lax.broadcasted_iota(jnp.int32, (t, tq), 0)
    col = lax.broadcasted_iota(jnp.int32, (t, tq), 1)

    def q_head(h, first_row=0):
        if moba:
            return qm_ref[h, first_row:, :]
        return q_ref[0, first_row:, h * HEAD_SLAB:(h + 1) * HEAD_SLAB]

    def k_block(kb, h):
        start = pl.multiple_of(kb * t, t)
        lanes = (h // 2) * LANES if moba else h * HEAD_SLAB
        return k_ref[0, pl.ds(start, t), lanes:lanes + LANES]

    if moba:
        lane = lax.broadcasted_iota(jnp.int32, (tq, LANES), 1)
        for h in range(nh):
            q2 = q_ref[0, :, (h // 2) * LANES:(h // 2 + 1) * LANES]
            keep = (lane < MOBA_DH) if h % 2 == 0 else (lane >= MOBA_DH)
            qm_ref[h] = jnp.where(keep, q2, jnp.zeros_like(q2))
        blk = lax.broadcasted_iota(jnp.int32, (nkb, tq), 0).astype(F32)
        own = (own0 + jnp.where(qcol >= t, 1, 0)).astype(F32)
        for h in range(nh):
            km = kmean_ref[0, :, (h // 2) * LANES:(h // 2 + 1) * LANES]
            g = lax.dot_general(km, qm_ref[h].astype(F32), (((1,), (1,)), ((), ())),
                                preferred_element_type=F32)
            g = jnp.where(blk < own, g, NEG_INF)
            bias = jnp.full((nkb, tq), NEG_INF, F32)
            for _ in range(MOBA_TOPK):
                best = jnp.max(g, axis=0, keepdims=True)
                hit = (g == best) & (best > NEG_INF)
                first = jnp.min(jnp.where(hit, blk, float(nkb)), axis=0, keepdims=True)
                pick = blk == first
                bias = jnp.where(pick, 0.0, bias)
                g = jnp.where(pick, NEG_INF, g)
            bias_ref[h] = jnp.where(blk == own, 0.0, bias)

    def stage_a(kb, s_ref, cm_ref):
        for h in range(nh):
            s = lax.dot_general(k_block(kb, h), q_head(h), (((1,), (1,)), ((), ())),
                                preferred_element_type=F32)
            s_ref[h] = s.astype(BF16)
            cm = jnp.max(s, axis=0, keepdims=True)
            if moba:
                cm = cm + bias_ref[h, pl.ds(kb, 1), :]
            cm_ref[h] = cm

    def stage_b(s_ref, cm_ref, p_ref, al_ref, kb=None):
        for h in range(nh):
            m_old = m_ref[h]
            m_new = jnp.maximum(m_old, cm_ref[h])
            m_ref[h] = m_new
            al_ref[h] = jnp.exp2(m_old - m_new)
            if moba and kb is not None:
                m_new = m_new - bias_ref[h, pl.ds(kb, 1), :]
            p_ref[h] = jnp.exp2(s_ref[h] - m_new.astype(BF16))

    ones_rows = jnp.ones((SUM_ROWS, t), BF16)

    def stage_c(kb, p_ref, al_ref):
        for h in range(nh):
            vt = jnp.concatenate([vt_ref[0, kb, h * dv:(h + 1) * dv, :], ones_rows], axis=0)
            acc_ref[h] = al_ref[h] * acc_ref[h] + jnp.dot(vt, p_ref[h], preferred_element_type=F32)

    m_ref[...] = jnp.full(m_ref.shape, MASK_FLOOR, F32)
    acc_ref[...] = jnp.zeros(acc_ref.shape, F32)

    stage_a(0, s0, cm0)
    tri = jnp.where(row <= col, 0.0, NEG_INF).astype(F32)[:, :t]
    for h in range(nh):
        s = lax.dot_general(k_block(own0, h), q_head(h), (((1,), (1,)), ((), ())),
                            preferred_element_type=F32)
        if moba:
            s = s + bias_ref[h, pl.ds(own0, 1), :]
        s = jnp.concatenate([s[:, :t] + tri, s[:, t:]], axis=1)
        s2[h] = s.astype(BF16)
        cm2[h] = jnp.max(s, axis=0, keepdims=True)
    for h in range(nh):
        s = lax.dot_general(k_block(own0 + 1, h), q_head(h, t), (((1,), (1,)), ((), ())),
                            preferred_element_type=F32) + tri
        s1[h, :, t:] = s.astype(BF16)
        cm1[h, :, t:] = jnp.max(s, axis=0, keepdims=True)
    stage_b(s2, cm2, p0, al0)
    for h in range(nh):
        m_old = m_ref[h, :, t:]
        m_new = jnp.maximum(m_old, cm1[h, :, t:])
        m_ref[h, :, t:] = m_new
        al1[h] = jnp.concatenate([jnp.ones((1, t), F32), jnp.exp2(m_old - m_new)], axis=1)
        p1[h, :, :t] = jnp.zeros((t, t), BF16)
        p1[h, :, t:] = jnp.exp2(s1[h, :, t:] - m_new.astype(BF16))
    stage_c(own0, p0, al0)

    def pair(up, carry):
        kb = Q_BLOCKS * up
        stage_a(kb + 1, s1, cm1)
        stage_b(s0, cm0, p0, al0, kb)
        stage_c(jnp.where(up == 0, own0 + 1, kb - 1), p1, al1)
        stage_a(kb + 2, s0, cm0)
        stage_b(s1, cm1, p1, al1, kb + 1)
        stage_c(kb, p0, al0)
        return carry

    lax.fori_loop(0, a, pair, 0)
    stage_c(jnp.where(a == 0, own0 + 1, own0 - 1), p1, al1)

    outs = []
    for h in range(nh):
        acc = acc_ref[h]
        outs.append(acc[:dv] * (1.0 / acc[dv:dv + 1]))
    o_ref[0] = (jnp.concatenate(outs, axis=0).T * z_ref[0].astype(F32)).astype(o_ref.dtype)


def _attention(q, k, vt, z, kmean=None):
    B, S, _ = q.shape
    t = ATTN_TILE
    tq = Q_BLOCKS * t
    nh = ATTN_HEADS
    nkb = S // t
    groups = vt.shape[2] // (nh * MLA_V)
    wq = q.shape[2] // groups
    dv = vt.shape[2] // (groups * nh)
    moba = kmean is not None
    assert S % tq == 0 and dv == MLA_V == MOBA_DH
    in_specs = [pl.BlockSpec((1, tq, wq), lambda b, g, i: (b, i, g)),
                pl.BlockSpec((1, S, wq), lambda b, g, i: (b, 0, g)),
                pl.BlockSpec((1, nkb, nh * dv, t), lambda b, g, i: (b, 0, g, 0)),
                pl.BlockSpec((1, tq, nh * dv), lambda b, g, i: (b, i, g))]
    args = [q, k, vt, z]
    row_vec = pltpu.VMEM((nh, 1, tq), F32)
    score_slot = pltpu.VMEM((nh, t, tq), BF16)
    scratch = [score_slot, score_slot, score_slot,
               pltpu.VMEM((nh, t, tq), BF16), pltpu.VMEM((nh, t, tq), BF16),
               row_vec, row_vec, row_vec, row_vec, row_vec,
               row_vec,
               pltpu.VMEM((nh, dv + SUM_ROWS, tq), F32)]
    if moba:
        in_specs.append(pl.BlockSpec((1, nkb, wq), lambda b, g, i: (b, 0, g)))
        args.append(kmean)
        scratch += [pltpu.VMEM((nh, nkb, tq), F32),
                    pltpu.VMEM((nh, tq, LANES), BF16)]
    return pl.pallas_call(
        functools.partial(_attn_kernel, moba=moba),
        grid=(B, groups, S // tq),
        in_specs=in_specs,
        out_specs=pl.BlockSpec((1, tq, nh * dv), lambda b, g, i: (b, i, g)),
        out_shape=jax.ShapeDtypeStruct((B, S, groups * nh * dv), BF16),
        scratch_shapes=scratch,
        compiler_params=pltpu.CompilerParams(
            dimension_semantics=("parallel", "parallel", "parallel"), vmem_limit_bytes=VMEM_LIMIT),
        name="moba_attn" if moba else "mla_attn",
    )(*args)


def _outproj_kernel(x_ref, mod_ref, oml_ref, omb_ref, ga_ref, gb_ref,
                    woa_ref, wob_ref, wout_ref, gpost_ref, o_ref):
    y_a = jnp.dot(oml_ref[0], woa_ref[...], preferred_element_type=F32)
    y_b = jnp.dot(omb_ref[0], wob_ref[...], preferred_element_type=F32)
    u = ga_ref[0].astype(F32) * y_a + gb_ref[0].astype(F32) * y_b
    y = jnp.dot(u.astype(BF16), wout_ref[...], preferred_element_type=F32)
    gate = mod_ref[0, 2:3, :]
    o_ref[0] = x_ref[0] + gate * _rms(y, gpost_ref[...])


def _outproj(x, mod, o_mla, o_mb, g_a, g_b, woa, wob, wout, g_post):
    B, S, D = x.shape
    tm = TOKEN_TILE
    width = o_mla.shape[2]

    def const(shape):
        return pl.BlockSpec(shape, lambda b, i: (0,) * len(shape), pipeline_mode=pl.Buffered(1))

    def rows(n):
        return pl.BlockSpec((1, tm, n), lambda b, i: (b, i, 0))

    return pl.pallas_call(
        _outproj_kernel,
        grid=(B, S // tm),
        in_specs=[rows(D), pl.BlockSpec((1, 3, D), lambda b, i: (b, 0, 0)),
                  rows(width), rows(width), rows(D), rows(D),
                  const(woa.shape), const(wob.shape), const(wout.shape), const((1, D))],
        out_specs=rows(D),
        out_shape=jax.ShapeDtypeStruct((B, S, D), x.dtype),
        compiler_params=pltpu.CompilerParams(
            dimension_semantics=("parallel", "parallel"), vmem_limit_bytes=VMEM_LIMIT),
        name="outproj",
    )(x, mod, o_mla, o_mb, g_a, g_b, woa, wob, wout, g_post)


def kernel(x, c, positions, w_ada, b_ada, g_pre, g_post, w_in, g_q_lat, w_uq, g_kv_lat, w_ukv,
           w_o_mla, w_o_moba, b_merge, w_out):
    B, S, D = x.shape
    depth = w_ada.shape[0]
    assert S % TOKEN_TILE == 0 and S % ROPE_TILE == 0 and TOKEN_TILE % ATTN_TILE == 0
    assert w_in.shape[2] == (MLA_Q_RANK + MLA_KV_RANK + MLA_ROPE + 5 * 512 + 2 * D)
    tabs = _rope_tables(positions)
    for l in range(depth):
        mod = _adaln(c, w_ada[l], b_ada[l]).reshape(B, 3, D)
        w, wuq, wuk, wuv = _prep_in_weights(w_in[l], w_uq[l], w_ukv[l])
        (q_ml, k_ml, vt_ml, z_ml, q_mb, k_mb, vt_mb, z_mb, g_a, g_b, kmean) = _inproj(
            x, mod, g_pre[l][None], w, g_q_lat[l][None], wuq, g_kv_lat[l][None], wuk, wuv,
            b_merge[l][None], tabs)
        o_ml = _attention(q_ml, k_ml, vt_ml, z_ml)
        o_mb = _attention(q_mb, k_mb, vt_mb, z_mb, kmean.reshape(B, S // ATTN_TILE, -1))
        x = _outproj(x, mod, o_ml, o_mb, g_a, g_b,
                     w_o_mla[l].astype(BF16), w_o_moba[l].astype(BF16), w_out[l].astype(BF16),
                     g_post[l][None])
    return x
```

```python
import functools

import jax
import jax.numpy as jnp
from jax import lax
from jax.experimental import pallas as pl
from jax.experimental.pallas import tpu as pltpu

EPS = 1e-6
ROPE_THETA = 10000.0
MLA_HEADS = 8
MLA_NOPE = 64
MLA_ROPE = 32
MLA_V = 64
MLA_Q_RANK = 256
MLA_KV_RANK = 128
MOBA_HEADS = 8
MOBA_DH = 64
MOBA_BLOCK = 256
MOBA_TOPK = 3

LANES = 128
HEAD_SLAB = LANES
ATTN_TILE = MOBA_BLOCK
Q_BLOCKS = 2
ATTN_HEADS = 4
MASK_FLOOR = -1e30
SUM_ROWS = 16
LOG2E = 1.4426950408889634
TOKEN_TILE = 512
OUT_TILE = 1024
ROPE_TILE = 512
VMEM_LIMIT = 56 * 1024 * 1024

F32 = jnp.float32
BF16 = jnp.bfloat16
NEG_INF = float("-inf")


def _sigmoid(t):
    return 1.0 / (1.0 + jnp.exp(-t))


def _silu(t):
    return t * _sigmoid(t)


def _adaln_kernel(c_ref, w_ref, b_ref, o_ref):
    c = c_ref[...]
    o_ref[...] = jnp.dot(_silu(c), w_ref[...], preferred_element_type=F32) + b_ref[...]


def _adaln(c, w_ada, b_ada):
    B, D = c.shape
    n = w_ada.shape[1] // D
    return pl.pallas_call(
        _adaln_kernel,
        grid=(n,),
        in_specs=[pl.BlockSpec((B, D), lambda j: (0, 0)),
                  pl.BlockSpec((D, D), lambda j: (0, j)),
                  pl.BlockSpec((1, D), lambda j: (0, j))],
        out_specs=pl.BlockSpec((B, D), lambda j: (0, j)),
        out_shape=jax.ShapeDtypeStruct((B, n * D), F32),
        compiler_params=pltpu.CompilerParams(vmem_limit_bytes=VMEM_LIMIT),
        name="adaln",
    )(c, w_ada, b_ada.reshape(1, -1))


def _rope_kernel(pos_ref, fmb_ref, fml_ref, cmb_ref, smb_ref, cml_ref, sml_ref):
    nb, ts = pos_ref.shape
    ones = jnp.ones((2 * MLA_ROPE, ts), F32)
    zeros64 = jnp.zeros((2 * MLA_ROPE, ts), F32)
    zeros32 = jnp.zeros((MLA_ROPE, ts), F32)
    for b in range(nb):
        pos = pos_ref[b:b + 1, :].astype(F32)
        amb = fmb_ref[...] * pos
        cb, sb = jnp.cos(amb), jnp.sin(amb)
        cmb_ref[b] = jnp.concatenate([cb, cb, cb, cb], axis=0).T
        smb_ref[b] = jnp.concatenate([-sb, sb, -sb, sb], axis=0).T
        aml = fml_ref[...] * pos
        ca, sa = jnp.cos(aml), jnp.sin(aml)
        cml_ref[b] = jnp.concatenate([ones, ca, ca, zeros32], axis=0).T
        sml_ref[b] = jnp.concatenate([zeros64, -sa, sa, zeros32], axis=0).T


def _rope_tables(positions):
    B, S = positions.shape
    ts = ROPE_TILE
    half_mb, half_ml = MOBA_DH // 2, MLA_ROPE // 2
    fmb = (ROPE_THETA ** (-jnp.arange(half_mb, dtype=F32) / half_mb)).reshape(half_mb, 1)
    fml = (ROPE_THETA ** (-jnp.arange(half_ml, dtype=F32) / half_ml)).reshape(half_ml, 1)
    tab = jax.ShapeDtypeStruct((B, S, LANES), F32)
    tab_spec = pl.BlockSpec((B, ts, LANES), lambda j: (0, j, 0))
    return pl.pallas_call(
        _rope_kernel,
        grid=(S // ts,),
        in_specs=[pl.BlockSpec((B, ts), lambda j: (0, j)),
                  pl.BlockSpec((half_mb, 1), lambda j: (0, 0)),
                  pl.BlockSpec((half_ml, 1), lambda j: (0, 0))],
        out_specs=[tab_spec] * 4,
        out_shape=[tab] * 4,
        compiler_params=pltpu.CompilerParams(vmem_limit_bytes=VMEM_LIMIT),
        name="rope_tables",
    )(positions, fmb, fml)


def _rope_slab(t, cos, sin, half):
    lane = lax.broadcasted_iota(jnp.int32, t.shape, 1)
    first = (lane & (2 * half - 1)) < half
    partner = jnp.where(first, pltpu.roll(t, LANES - half, 1), pltpu.roll(t, half, 1))
    return t * cos + partner * sin


def _rms(t, g):
    return t * lax.rsqrt(jnp.mean(t * t, axis=-1, keepdims=True) + EPS) * g


def _inproj_kernel(x_ref, mod_ref, gpre_ref, w_ref, gq_ref, wuq_ref, gkv_ref, wuk_ref, wuv_ref,
                   bm_ref, cmb_ref, smb_ref, cml_ref, sml_ref,
                   qml_ref, kml_ref, vtml_ref, zml_ref, qmb_ref, kmb_ref, vtmb_ref, zmb_ref,
                   ga_ref, gb_ref, kmean_ref):
    tm = x_ref.shape[1]
    nsub = tm // ATTN_TILE
    x = x_ref[0]
    shift = mod_ref[0, 0:1, :]
    scale = mod_ref[0, 1:2, :]
    h = (_rms(x, gpre_ref[...]) * (1.0 + scale) + shift).astype(BF16)

    def proj(lo, hi):
        return jnp.dot(h, w_ref[:, lo:hi], preferred_element_type=F32)

    cmb, smb = cmb_ref[0], smb_ref[0]
    cml, sml = cml_ref[0], sml_ref[0]

    g0 = proj(0, 512)
    q_scale = float((MLA_NOPE + MLA_ROPE) ** -0.5) * LOG2E
    qn = _rms(g0[:, :MLA_Q_RANK], gq_ref[...] * q_scale).astype(BF16)
    kvn = _rms(g0[:, MLA_Q_RANK:MLA_Q_RANK + MLA_KV_RANK], gkv_ref[...]).astype(BF16)
    kpe = _rope_slab(g0[:, 384:512], cml, sml, MLA_ROPE // 2)
    q = jnp.dot(qn, wuq_ref[...], preferred_element_type=F32)
    kn = jnp.dot(kvn, wuk_ref[...], preferred_element_type=F32)
    for hd in range(MLA_HEADS):
        sl = slice(hd * HEAD_SLAB, (hd + 1) * HEAD_SLAB)
        qml_ref[0, :, sl] = _rope_slab(q[:, sl], cml, sml, MLA_ROPE // 2).astype(BF16)
        kml_ref[0, :, sl] = (kn[:, sl] + kpe).astype(BF16)
    v = jnp.dot(kvn, wuv_ref[...], preferred_element_type=F32)
    for r in range(nsub):
        vtml_ref[0, r] = v[r * ATTN_TILE:(r + 1) * ATTN_TILE, :].T.astype(BF16)
    zml_ref[0] = _silu(proj(512, 1024)).astype(BF16)

    qm = proj(1024, 1536)
    km = proj(1536, 2048)
    mb_scale = float(MOBA_DH ** -0.5) * LOG2E
    for cidx in range(MOBA_HEADS * MOBA_DH // LANES):
        sl = slice(cidx * LANES, (cidx + 1) * LANES)
        qmb_ref[0, :, sl] = (_rope_slab(qm[:, sl], cmb, smb, MOBA_DH // 2) * mb_scale).astype(BF16)
        kr = _rope_slab(km[:, sl], cmb, smb, MOBA_DH // 2)
        kmb_ref[0, :, sl] = kr.astype(BF16)
        for r in range(nsub):
            kmean_ref[0, r, :, sl] = jnp.mean(kr[r * ATTN_TILE:(r + 1) * ATTN_TILE, :], axis=0, keepdims=True)
    vm = proj(2048, 2560)
    for r in range(nsub):
        vtmb_ref[0, r] = vm[r * ATTN_TILE:(r + 1) * ATTN_TILE, :].T.astype(BF16)
    zmb_ref[0] = _silu(proj(2560, 3072)).astype(BF16)

    D = x_ref.shape[2]
    ga_ref[0] = _sigmoid(proj(3072, 3072 + D) + bm_ref[:, :D]).astype(BF16)
    gb_ref[0] = _sigmoid(proj(3072 + D, 3072 + 2 * D) + bm_ref[:, D:]).astype(BF16)


def _prep_in_weights(w_in, w_uq, w_ukv):
    D = w_in.shape[0]
    o = 0
    cols = {}
    for name, n in (("q_lat", MLA_Q_RANK), ("kv_lat", MLA_KV_RANK), ("k_rope", MLA_ROPE),
                    ("z_mla", 512), ("q_mb", 512), ("k_mb", 512), ("v_mb", 512), ("z_mb", 512),
                    ("merge", 2 * D)):
        cols[name] = w_in[:, o:o + n]
        o += n
    zpad = lambda n: jnp.zeros((D, n), w_in.dtype)
    w = jnp.concatenate([cols["q_lat"], cols["kv_lat"],
                         zpad(MLA_NOPE), cols["k_rope"], zpad(HEAD_SLAB - MLA_NOPE - MLA_ROPE),
                         cols["z_mla"], cols["q_mb"], cols["k_mb"], cols["v_mb"], cols["z_mb"],
                         cols["merge"]], axis=1).astype(BF16)
    dq = MLA_NOPE + MLA_ROPE
    wuq = w_uq.reshape(MLA_Q_RANK, MLA_HEADS, dq)
    wuq = jnp.pad(wuq, ((0, 0), (0, 0), (0, HEAD_SLAB - dq))).reshape(MLA_Q_RANK, MLA_HEADS * HEAD_SLAB)
    wukv = w_ukv.reshape(MLA_KV_RANK, MLA_HEADS, MLA_NOPE + MLA_V)
    wuk = jnp.pad(wukv[:, :, :MLA_NOPE], ((0, 0), (0, 0), (0, HEAD_SLAB - MLA_NOPE)))
    wuk = wuk.reshape(MLA_KV_RANK, MLA_HEADS * HEAD_SLAB)
    wuv = wukv[:, :, MLA_NOPE:].reshape(MLA_KV_RANK, MLA_HEADS * MLA_V)
    return w, wuq.astype(BF16), wuk.astype(BF16), wuv.astype(BF16)


def _inproj(x, mod, g_pre, w, g_q, wuq, g_kv, wuk, wuv, b_merge, tabs):
    B, S, D = x.shape
    tm = TOKEN_TILE
    nkb = S // ATTN_TILE
    nsub = tm // ATTN_TILE
    wide = MLA_HEADS * HEAD_SLAB
    mbw = MOBA_HEADS * MOBA_DH

    def const(shape):
        return pl.BlockSpec(shape, lambda b, i: (0,) * len(shape), pipeline_mode=pl.Buffered(1))

    def rows(n):
        return pl.BlockSpec((1, tm, n), lambda b, i: (b, i, 0))

    vt_spec = pl.BlockSpec((1, nsub, mbw, ATTN_TILE), lambda b, i: (b, i, 0, 0))
    bf = lambda *s: jax.ShapeDtypeStruct(s, BF16)
    out_shape = [bf(B, S, wide), bf(B, S, wide), bf(B, nkb, mbw, ATTN_TILE), bf(B, S, mbw),
                 bf(B, S, mbw), bf(B, S, mbw), bf(B, nkb, mbw, ATTN_TILE), bf(B, S, mbw),
                 bf(B, S, D), bf(B, S, D),
                 jax.ShapeDtypeStruct((B, nkb, 1, mbw), F32)]
    out_specs = [rows(wide), rows(wide), vt_spec, rows(mbw),
                 rows(mbw), rows(mbw), vt_spec, rows(mbw),
                 rows(D), rows(D),
                 pl.BlockSpec((1, nsub, 1, mbw), lambda b, i: (b, i, 0, 0))]
    in_specs = [rows(D),
                pl.BlockSpec((1, 3, D), lambda b, i: (b, 0, 0)),
                const((1, D)), const(w.shape), const((1, MLA_Q_RANK)), const(wuq.shape),
                const((1, MLA_KV_RANK)), const(wuk.shape), const(wuv.shape), const((1, 2 * D)),
                rows(LANES), rows(LANES), rows(LANES), rows(LANES)]
    return pl.pallas_call(
        _inproj_kernel,
        grid=(B, S // tm),
        in_specs=in_specs,
        out_specs=out_specs,
        out_shape=out_shape,
        compiler_params=pltpu.CompilerParams(
            dimension_semantics=("parallel", "parallel"), vmem_limit_bytes=VMEM_LIMIT),
        name="inproj",
    )(x, mod, g_pre, w, g_q, wuq, g_kv, wuk, wuv, b_merge, *tabs)


def _attn_kernel(*refs, moba):
    if moba:
        (q_ref, k_ref, vt_ref, z_ref, kmean_ref, o_ref,
         s0, s1, s2, p0, p1, cm0, cm1, cm2, al0, al1, m_ref, acc_ref, bias_ref, qm_ref) = refs
    else:
        (q_ref, k_ref, vt_ref, z_ref, o_ref,
         s0, s1, s2, p0, p1, cm0, cm1, cm2, al0, al1, m_ref, acc_ref) = refs
    t = ATTN_TILE
    tq = Q_BLOCKS * t
    nh = s0.shape[0]
    a = pl.program_id(2)
    own0 = Q_BLOCKS * a
    dv = vt_ref.shape[2] // nh
    nkb = vt_ref.shape[1]
    qcol = lax.broadcasted_iota(jnp.int32, (1, tq), 1)
    row = lax.broadcasted_iota(jnp.int32, (t, tq), 0)
    col = lax.broadcasted_iota(jnp.int32, (t, tq), 1)

    def q_head(h, first_row=0):
        if moba:
            return qm_ref[h, first_row:, :]
        return q_ref[0, first_row:, h * HEAD_SLAB:(h + 1) * HEAD_SLAB]

    def k_block(kb, h):
        start = pl.multiple_of(kb * t, t)
        lanes = (h // 2) * LANES if moba else h * HEAD_SLAB
        return k_ref[0, pl.ds(start, t), lanes:lanes + LANES]

    if moba:
        lane = lax.broadcasted_iota(jnp.int32, (tq, LANES), 1)
        for h in range(nh):
            q2 = q_ref[0, :, (h // 2) * LANES:(h // 2 + 1) * LANES]
            keep = (lane < MOBA_DH) if h % 2 == 0 else (lane >= MOBA_DH)
            qm_ref[h] = jnp.where(keep, q2, jnp.zeros_like(q2))
        blk = lax.broadcasted_iota(jnp.int32, (nkb, tq), 0).astype(F32)
        own = (own0 + jnp.where(qcol >= t, 1, 0)).astype(F32)
        for h in range(nh):
            km = kmean_ref[0, :, (h // 2) * LANES:(h // 2 + 1) * LANES]
            g = lax.dot_general(km, qm_ref[h].astype(F32), (((1,), (1,)), ((), ())),
                                preferred_element_type=F32)
            g = jnp.where(blk < own, g, NEG_INF)
            bias = jnp.full((nkb, tq), NEG_INF, F32)
            for _ in range(MOBA_TOPK):
                best = jnp.max(g, axis=0, keepdims=True)
                hit = (g == best) & (best > NEG_INF)
                first = jnp.min(jnp.where(hit, blk, float(nkb)), axis=0, keepdims=True)
                pick = blk == first
                bias = jnp.where(pick, 0.0, bias)
                g = jnp.where(pick, NEG_INF, g)
            bias_ref[h] = jnp.where(blk == own, 0.0, bias)

    def stage_a(kb, s_ref, cm_ref):
        for h in range(nh):
            s = lax.dot_general(k_block(kb, h), q_head(h), (((1,), (1,)), ((), ())),
                                preferred_element_type=F32)
            s_ref[h] = s
            cm = jnp.max(s, axis=0, keepdims=True)
            if moba:
                cm = cm + bias_ref[h, pl.ds(kb, 1), :]
            cm_ref[h] = cm

    def stage_b(s_ref, cm_ref, p_ref, al_ref, kb=None):
        for h in range(nh):
            m_old = m_ref[h]
            m_new = jnp.maximum(m_old, cm_ref[h])
            m_ref[h] = m_new
            al_ref[h] = jnp.exp2(m_old - m_new)
            if moba and kb is not None:
                m_new = m_new - bias_ref[h, pl.ds(kb, 1), :]
            p_ref[h] = jnp.exp2(s_ref[h] - m_new).astype(BF16)

    ones_rows = jnp.ones((SUM_ROWS, t), BF16)

    def stage_c(kb, p_ref, al_ref):
        for h in range(nh):
            vt = jnp.concatenate([vt_ref[0, kb, h * dv:(h + 1) * dv, :], ones_rows], axis=0)
            acc_ref[h] = al_ref[h] * acc_ref[h] + jnp.dot(vt, p_ref[h], preferred_element_type=F32)

    m_ref[...] = jnp.full(m_ref.shape, MASK_FLOOR, F32)
    acc_ref[...] = jnp.zeros(acc_ref.shape, F32)

    stage_a(0, s0, cm0)
    tri = jnp.where(row <= col, 0.0, NEG_INF).astype(F32)[:, :t]
    for h in range(nh):
        s = lax.dot_general(k_block(own0, h), q_head(h), (((1,), (1,)), ((), ())),
                            preferred_element_type=F32)
        if moba:
            s = s + bias_ref[h, pl.ds(own0, 1), :]
        s = jnp.concatenate([s[:, :t] + tri, s[:, t:]], axis=1)
        s2[h] = s
        cm2[h] = jnp.max(s, axis=0, keepdims=True)
    for h in range(nh):
        s = lax.dot_general(k_block(own0 + 1, h), q_head(h, t), (((1,), (1,)), ((), ())),
                            preferred_element_type=F32) + tri
        s1[h, :, t:] = s
        cm1[h, :, t:] = jnp.max(s, axis=0, keepdims=True)
    stage_b(s2, cm2, p0, al0)
    for h in range(nh):
        m_old = m_ref[h, :, t:]
        m_new = jnp.maximum(m_old, cm1[h, :, t:])
        m_ref[h, :, t:] = m_new
        al1[h] = jnp.concatenate([jnp.ones((1, t), F32), jnp.exp2(m_old - m_new)], axis=1)
        p1[h, :, :t] = jnp.zeros((t, t), BF16)
        p1[h, :, t:] = jnp.exp2(s1[h, :, t:] - m_new).astype(BF16)
    stage_c(own0, p0, al0)

    def pair(up, carry, look_ahead=True):
        kb = Q_BLOCKS * up
        stage_a(kb + 1, s1, cm1)
        stage_b(s0, cm0, p0, al0, kb)
        stage_c(jnp.where(up == 0, own0 + 1, kb - 1), p1, al1)
        if look_ahead:
            stage_a(kb + 2, s0, cm0)
        stage_b(s1, cm1, p1, al1, kb + 1)
        stage_c(kb, p0, al0)
        return carry

    lax.fori_loop(0, a - 1, pair, 0)

    @pl.when(a > 0)
    def _():
        pair(a - 1, 0, look_ahead=False)

    stage_c(jnp.where(a == 0, own0 + 1, own0 - 1), p1, al1)

    outs = []
    for h in range(nh):
        acc = acc_ref[h]
        outs.append(acc[:dv] * (1.0 / acc[dv:dv + 1]))
    o_ref[0] = (jnp.concatenate(outs, axis=0).T * z_ref[0].astype(F32)).astype(o_ref.dtype)


def _attention(q, k, vt, z, kmean=None):
    B, S, _ = q.shape
    t = ATTN_TILE
    tq = Q_BLOCKS * t
    nh = ATTN_HEADS
    nkb = S // t
    groups = vt.shape[2] // (nh * MLA_V)
    wq = q.shape[2] // groups
    dv = vt.shape[2] // (groups * nh)
    moba = kmean is not None
    assert S % tq == 0 and dv == MLA_V == MOBA_DH
    in_specs = [pl.BlockSpec((1, tq, wq), lambda b, g, i: (b, i, g)),
                pl.BlockSpec((1, S, wq), lambda b, g, i: (b, 0, g)),
                pl.BlockSpec((1, nkb, nh * dv, t), lambda b, g, i: (b, 0, g, 0)),
                pl.BlockSpec((1, tq, nh * dv), lambda b, g, i: (b, i, g))]
    args = [q, k, vt, z]
    row_vec = pltpu.VMEM((nh, 1, tq), F32)
    score_slot = pltpu.VMEM((nh, t, tq), F32)
    scratch = [score_slot, score_slot, score_slot,
               pltpu.VMEM((nh, t, tq), BF16), pltpu.VMEM((nh, t, tq), BF16),
               row_vec, row_vec, row_vec, row_vec, row_vec,
               row_vec,
               pltpu.VMEM((nh, dv + SUM_ROWS, tq), F32)]
    if moba:
        in_specs.append(pl.BlockSpec((1, nkb, wq), lambda b, g, i: (b, 0, g)))
        args.append(kmean)
        scratch += [pltpu.VMEM((nh, nkb, tq), F32),
                    pltpu.VMEM((nh, tq, LANES), BF16)]
    return pl.pallas_call(
        functools.partial(_attn_kernel, moba=moba),
        grid=(B, groups, S // tq),
        in_specs=in_specs,
        out_specs=pl.BlockSpec((1, tq, nh * dv), lambda b, g, i: (b, i, g)),
        out_shape=jax.ShapeDtypeStruct((B, S, groups * nh * dv), BF16),
        scratch_shapes=scratch,
        compiler_params=pltpu.CompilerParams(
            dimension_semantics=("parallel", "parallel", "parallel"), vmem_limit_bytes=VMEM_LIMIT),
        name="moba_attn" if moba else "mla_attn",
    )(*args)


def _outproj_kernel(x_ref, mod_ref, oml_ref, omb_ref, ga_ref, gb_ref,
                    woa_ref, wob_ref, wout_ref, gpost_ref, o_ref):
    y_a = jnp.dot(oml_ref[0], woa_ref[...], preferred_element_type=F32)
    y_b = jnp.dot(omb_ref[0], wob_ref[...], preferred_element_type=F32)
    u = ga_ref[0].astype(F32) * y_a + gb_ref[0].astype(F32) * y_b
    y = jnp.dot(u.astype(BF16), wout_ref[...], preferred_element_type=F32)
    gate = mod_ref[0, 2:3, :]
    o_ref[0] = x_ref[0] + gate * _rms(y, gpost_ref[...])


def _outproj(x, mod, o_mla, o_mb, g_a, g_b, woa, wob, wout, g_post):
    B, S, D = x.shape
    tm = OUT_TILE
    assert S % tm == 0
    width = o_mla.shape[2]

    def const(shape):
        return pl.BlockSpec(shape, lambda b, i: (0,) * len(shape), pipeline_mode=pl.Buffered(1))

    def rows(n):
        return pl.BlockSpec((1, tm, n), lambda b, i: (b, i, 0))

    return pl.pallas_call(
        _outproj_kernel,
        grid=(B, S // tm),
        in_specs=[rows(D), pl.BlockSpec((1, 3, D), lambda b, i: (b, 0, 0)),
                  rows(width), rows(width), rows(D), rows(D),
                  const(woa.shape), const(wob.shape), const(wout.shape), const((1, D))],
        out_specs=rows(D),
        out_shape=jax.ShapeDtypeStruct((B, S, D), x.dtype),
        compiler_params=pltpu.CompilerParams(
            dimension_semantics=("parallel", "parallel"), vmem_limit_bytes=VMEM_LIMIT),
        name="outproj",
    )(x, mod, o_mla, o_mb, g_a, g_b, woa, wob, wout, g_post)


def kernel(x, c, positions, w_ada, b_ada, g_pre, g_post, w_in, g_q_lat, w_uq, g_kv_lat, w_ukv,
           w_o_mla, w_o_moba, b_merge, w_out):
    B, S, D = x.shape
    depth = w_ada.shape[0]
    assert S % TOKEN_TILE == 0 and S % ROPE_TILE == 0 and TOKEN_TILE % ATTN_TILE == 0
    assert w_in.shape[2] == (MLA_Q_RANK + MLA_KV_RANK + MLA_ROPE + 5 * 512 + 2 * D)
    tabs = _rope_tables(positions)
    for l in range(depth):
        mod = _adaln(c, w_ada[l], b_ada[l]).reshape(B, 3, D)
        w, wuq, wuk, wuv = _prep_in_weights(w_in[l], w_uq[l], w_ukv[l])
        (q_ml, k_ml, vt_ml, z_ml, q_mb, k_mb, vt_mb, z_mb, g_a, g_b, kmean) = _inproj(
            x, mod, g_pre[l][None], w, g_q_lat[l][None], wuq, g_kv_lat[l][None], wuk, wuv,
            b_merge[l][None], tabs)
        o_ml = _attention(q_ml, k_ml, vt_ml, z_ml)
        o_mb = _attention(q_mb, k_mb, vt_mb, z_mb, kmean.reshape(B, S // ATTN_TILE, -1))
        x = _outproj(x, mod, o_ml, o_mb, g_a, g_b,
                     w_o_mla[l].astype(BF16), w_o_moba[l].astype(BF16), w_out[l].astype(BF16),
                     g_post[l][None])
    return x
```
